```python
import math
import jax, jax.numpy as jnp
from jax import lax
import numpy as np

D_MODEL = 1024
BATCH = 8
SEQ = 2048
DEPTH = 2
DEC_BATCH = 32
DEC_SEQ = 8
PAST_LEN = 16384
PAGE_SIZE = 128

SSM_WIDTH = D_MODEL // 2
SSM_GROUP = 16
SSM_GROUPS = SSM_WIDTH // SSM_GROUP
SSM_STATE = 64
SB_HEAD_DIM = 64
SB_HEADS = D_MODEL // 128
SB_WIDTH = SB_HEADS * SB_HEAD_DIM
SB_BLOCK = 128
RET_HEAD_DIM = 128
RET_HEADS = D_MODEL // 256
RET_WIDTH = RET_HEADS * RET_HEAD_DIM
RET_CHUNK = 128
ROPE_BASE = 10000.0
D_FF = 4 * D_MODEL
N_BRANCH = 3
EPS = 1e-6
IN_WIDTH = SSM_WIDTH + 3 * SB_WIDTH + 4 * RET_WIDTH
IN_SPLITS = (SSM_WIDTH,
             SSM_WIDTH + SB_WIDTH,
             SSM_WIDTH + 2 * SB_WIDTH,
             SSM_WIDTH + 3 * SB_WIDTH,
             SSM_WIDTH + 3 * SB_WIDTH + RET_WIDTH,
             SSM_WIDTH + 3 * SB_WIDTH + 2 * RET_WIDTH,
             SSM_WIDTH + 3 * SB_WIDTH + 3 * RET_WIDTH)

kernel_name = "hybrid_s5_stickbreak_retention_step"


def rms_norm(x, g):
    xf = x.astype(jnp.float32)
    y = xf * lax.rsqrt(jnp.mean(xf * xf, axis=-1, keepdims=True) + EPS)
    return (y * g.astype(jnp.float32)).astype(x.dtype)


def head_group_norm(o, g):
    mu = jnp.mean(o, axis=-1, keepdims=True)
    var = jnp.mean(jnp.square(o - mu), axis=-1, keepdims=True)
    y = (o - mu) * lax.rsqrt(var + EPS)
    return y.reshape(o.shape[0], o.shape[1], -1) * g.astype(jnp.float32)


def rotary(x, pos):
    xf = x.astype(jnp.float32)
    half = xf.shape[-1] // 2
    inv_freq = ROPE_BASE ** (-jnp.arange(half, dtype=jnp.float32) / half)
    ang = pos.astype(jnp.float32)[:, None] * inv_freq[None, :]
    cos = jnp.cos(ang)[None, :, None, :]
    sin = jnp.sin(ang)[None, :, None, :]
    x1, x2 = xf[..., :half], xf[..., half:]
    return jnp.concatenate([x1 * cos - x2 * sin, x1 * sin + x2 * cos], axis=-1)


def _linear_recurrence_combine(left, right):
    a1, b1 = left
    a2, b2 = right
    return a1 * a2, a2 * b1 + b2


def s5_mixer(u, x0_re, x0_im, lam_re, lam_im, log_dt, b_re, b_im, c_re, c_im, d):
    f32 = jnp.float32
    lam = lax.complex(lam_re.astype(f32), lam_im.astype(f32))
    dt = jnp.exp(log_dt.astype(f32))[:, None]
    lam_bar = jnp.exp(lam * dt)
    b_bar = ((lam_bar - 1.0) / lam)[..., None] * lax.complex(b_re.astype(f32), b_im.astype(f32))
    c = lax.complex(c_re.astype(f32), c_im.astype(f32))
    uf = u.astype(f32)
    bu = jnp.einsum('blgn,gpn->blgp', uf.astype(jnp.complex64), b_bar)
    x0 = lax.complex(x0_re.astype(f32), x0_im.astype(f32))
    bu = bu.at[:, 0].add(lam_bar[None] * x0)
    a = jnp.broadcast_to(lam_bar, bu.shape)
    _, xs = lax.associative_scan(_linear_recurrence_combine, (a, bu), axis=1)
    y = jnp.real(jnp.einsum('blgp,gnp->blgn', xs, c)) + d.astype(f32) * uf
    x_last = xs[:, -1]
    return y, jnp.real(x_last), jnp.imag(x_last)


def stick_breaking(q, k, v, bias, start):
    lq = q.shape[1]
    scale = SB_HEAD_DIM ** -0.5
    bias_f = bias.astype(jnp.float32)[None, :, None, None]
    outs = []
    for b0 in range(0, lq, SB_BLOCK):
        qb = q[:, b0:b0 + SB_BLOCK]
        nq = qb.shape[1]
        kend = start + b0 + nq
        kb = k[:, :kend]
        vb = v[:, :kend]
        z = jnp.einsum('bqhd,bkhd->bhqk', qb, kb, preferred_element_type=jnp.float32) * scale + bias_f
        t_pos = start + b0 + jnp.arange(nq)
        s_pos = jnp.arange(kend)
        mask = s_pos[None, :] < t_pos[:, None]
        log_stay = jnp.where(mask, jax.nn.log_sigmoid(-z), 0.0)
        log_after = lax.cumsum(log_stay, axis=3, reverse=True) - log_stay
        att = jnp.where(mask, jnp.exp(jax.nn.log_sigmoid(z) + log_after), 0.0)
        outs.append(jnp.einsum('bhqk,bkhd->bqhd', att, vb.astype(jnp.float32)))
    return jnp.concatenate(outs, axis=1)


def retention(q, k, v, s0):
    bsz, seq_len, n_heads, dv = v.shape
    c = min(RET_CHUNK, seq_len)
    n_chunks = seq_len // c
    log_g = jnp.log1p(-jnp.exp2(-5.0 - jnp.arange(n_heads, dtype=jnp.float32)))
    i = jnp.arange(c, dtype=jnp.float32)
    diff = i[:, None] - i[None, :]
    dmask = jnp.where(diff[None] >= 0, jnp.exp(jnp.maximum(diff, 0.0)[None] * log_g[:, None, None]), 0.0)
    xi = jnp.exp((i[:, None] + 1.0) * log_g[None, :])
    zeta = jnp.exp((c - 1.0 - i)[:, None] * log_g[None, :])
    g_chunk = jnp.exp(c * log_g)

    def to_chunks(t):
        return jnp.moveaxis(t.reshape(bsz, n_chunks, c, *t.shape[2:]), 1, 0)

    def step(s, inp):
        qc, kc, vc = inp
        inner = jnp.einsum('bihd,bjhd->bhij', qc, kc) * dmask[None]
        o = (jnp.einsum('bhij,bjhe->bihe', inner, vc)
             + jnp.einsum('bihd,bhde->bihe', qc, s) * xi[None, :, :, None])
        s = s * g_chunk[None, :, None, None] + jnp.einsum('bjhd,bjhe->bhde', kc * zeta[None, :, :, None], vc)
        return s, o

    s_last, o = lax.scan(step, s0, (to_chunks(q), to_chunks(k), to_chunks(v)))
    o = jnp.moveaxis(o, 0, 1).reshape(bsz, seq_len, n_heads, dv)
    return o, s_last


def trunk_layer(x, past_k, past_v, ssm_re0, ssm_im0, ret0,
                norm_mix, w_in, sb_q_norm, sb_k_norm, sb_logit_bias, ssm_lambda_re, ssm_lambda_im, ssm_log_dt,
                ssm_b_re, ssm_b_im, ssm_c_re, ssm_c_im, ssm_d, ssm_w_glu, ret_norm,
                w_branch_ssm, w_branch_sb, w_branch_ret, w_gate, b_gate, w_o,
                norm_ffn, w_ff_up, w_ff_down):
    f32 = jnp.float32
    bsz, seq_len, _ = x.shape
    start = 0 if past_k is None else past_k.shape[1]
    h = rms_norm(x, norm_mix)
    proj = h @ w_in
    u, sb_q, sb_k, sb_v, r_q, r_k, r_v, r_g = jnp.split(proj, IN_SPLITS, axis=-1)

    y_ssm, ssm_re, ssm_im = s5_mixer(u.reshape(bsz, seq_len, SSM_GROUPS, SSM_GROUP), ssm_re0, ssm_im0,
                                     ssm_lambda_re, ssm_lambda_im, ssm_log_dt,
                                     ssm_b_re, ssm_b_im, ssm_c_re, ssm_c_im, ssm_d)
    y_ssm = jax.nn.gelu(y_ssm.reshape(bsz, seq_len, SSM_WIDTH)).astype(x.dtype)
    glu_a, glu_b = jnp.split(y_ssm @ ssm_w_glu, 2, axis=-1)
    out_a = glu_a * jax.nn.sigmoid(glu_b)

    heads = (bsz, seq_len, SB_HEADS, SB_HEAD_DIM)
    q = rms_norm(sb_q.reshape(heads), sb_q_norm)
    k = rms_norm(sb_k.reshape(heads), sb_k_norm)
    v = sb_v.reshape(heads)
    if past_k is None:
        k_all, v_all = k, v
    else:
        k_all = jnp.concatenate([past_k.astype(k.dtype), k], axis=1)
        v_all = jnp.concatenate([past_v.astype(v.dtype), v], axis=1)
    out_b = stick_breaking(q, k_all, v_all, sb_logit_bias, start).reshape(bsz, seq_len, SB_WIDTH).astype(x.dtype)

    pos = start + jnp.arange(seq_len, dtype=jnp.int32)
    rh = (bsz, seq_len, RET_HEADS, RET_HEAD_DIM)
    rq = rotary(r_q.reshape(rh), pos)
    rk = rotary(r_k.reshape(rh), pos) * (RET_HEAD_DIM ** -0.5)
    ret_o, ret_state = retention(rq, rk, r_v.reshape(rh).astype(f32), ret0.astype(f32))
    out_c = (jax.nn.silu(r_g.astype(f32)) * head_group_norm(ret_o, ret_norm)).astype(x.dtype)

    gates = jax.nn.sigmoid((h @ w_gate + b_gate).astype(f32)).astype(x.dtype)
    g_a, g_b, g_c = jnp.split(gates, N_BRANCH, axis=-1)
    merged = (g_a * (out_a @ w_branch_ssm)
              + g_b * (out_b @ w_branch_sb)
              + g_c * (out_c @ w_branch_ret))
    x = x + merged @ w_o

    h2 = rms_norm(x, norm_ffn)
    x = x + jnp.square(jax.nn.relu(h2 @ w_ff_up)) @ w_ff_down
    return x, k, v, ssm_re, ssm_im, ret_state


def setup_inputs(seed: int = 0) -> dict:
    key = jax.random.key(seed)
    ks = jax.random.split(key, 40)
    f32 = jnp.float32

    def nrm(k, shape, scale):
        return jax.random.normal(k, shape, f32) * scale

    def gain(k, shape):
        return 1.0 + 0.01 * jax.random.normal(k, shape, f32)

    n_pages = PAST_LEN // PAGE_SIZE
    n_used = DEC_BATCH * n_pages
    n_phys = n_used + max(1, n_used // 4)
    page_table = jax.random.permutation(ks[0], n_phys)[:n_used].reshape(DEC_BATCH, n_pages).astype(jnp.int32)

    x_prompt = nrm(ks[1], (BATCH, SEQ, D_MODEL), 1.0)
    x_sample = nrm(ks[2], (DEC_BATCH, DEC_SEQ, D_MODEL), 1.0)
    cache_k = nrm(ks[3], (DEPTH, n_phys, PAGE_SIZE, SB_HEADS, SB_HEAD_DIM), 1.0)
    cache_v = nrm(ks[4], (DEPTH, n_phys, PAGE_SIZE, SB_HEADS, SB_HEAD_DIM), 1.0)
    state_ssm_re = nrm(ks[5], (DEPTH, DEC_BATCH, SSM_GROUPS, SSM_STATE), 0.1)
    state_ssm_im = nrm(ks[6], (DEPTH, DEC_BATCH, SSM_GROUPS, SSM_STATE), 0.1)
    state_ret = nrm(ks[7], (DEPTH, DEC_BATCH, RET_HEADS, RET_HEAD_DIM, RET_HEAD_DIM), 0.5)

    ssm_lambda_re = -0.5 + nrm(ks[8], (DEPTH, SSM_GROUPS, SSM_STATE), 0.01)
    ssm_lambda_im = (math.pi * jnp.arange(SSM_STATE, dtype=f32))[None, None, :] + nrm(ks[9], (DEPTH, SSM_GROUPS, SSM_STATE), 0.01)
    ssm_log_dt = jax.random.uniform(ks[10], (DEPTH, SSM_GROUPS), f32, math.log(1e-3), math.log(1e-1))
    b_scale = (2.0 * SSM_GROUP) ** -0.5
    c_scale = (2.0 * SSM_STATE) ** -0.5

    return {
        "x_prompt": x_prompt,
        "x_sample": x_sample,
        "cache_k": cache_k,
        "cache_v": cache_v,
        "state_ssm_re": state_ssm_re,
        "state_ssm_im": state_ssm_im,
        "state_ret": state_ret,
        "page_table": page_table,
        "norm_mix": gain(ks[11], (DEPTH, D_MODEL)),
        "w_in": nrm(ks[12], (DEPTH, D_MODEL, IN_WIDTH), D_MODEL ** -0.5),
        "sb_q_norm": gain(ks[13], (DEPTH, SB_HEAD_DIM)),
        "sb_k_norm": gain(ks[14], (DEPTH, SB_HEAD_DIM)),
        "sb_logit_bias": jax.random.uniform(ks[31], (DEPTH, SB_HEADS), f32, -10.0, -6.0),
        "ssm_lambda_re": ssm_lambda_re,
        "ssm_lambda_im": ssm_lambda_im,
        "ssm_log_dt": ssm_log_dt,
        "ssm_b_re": nrm(ks[15], (DEPTH, SSM_GROUPS, SSM_STATE, SSM_GROUP), b_scale),
        "ssm_b_im": nrm(ks[16], (DEPTH, SSM_GROUPS, SSM_STATE, SSM_GROUP), b_scale),
        "ssm_c_re": nrm(ks[17], (DEPTH, SSM_GROUPS, SSM_GROUP, SSM_STATE), c_scale),
        "ssm_c_im": nrm(ks[18], (DEPTH, SSM_GROUPS, SSM_GROUP, SSM_STATE), c_scale),
        "ssm_d": nrm(ks[19], (DEPTH, SSM_GROUPS, SSM_GROUP), 1.0),
        "ssm_w_glu": nrm(ks[20], (DEPTH, SSM_WIDTH, 2 * SSM_WIDTH), SSM_WIDTH ** -0.5),
        "ret_norm": gain(ks[21], (DEPTH, RET_WIDTH)),
        "w_branch_ssm": nrm(ks[22], (DEPTH, SSM_WIDTH, D_MODEL), SSM_WIDTH ** -0.5),
        "w_branch_sb": nrm(ks[23], (DEPTH, SB_WIDTH, D_MODEL), SB_WIDTH ** -0.5),
        "w_branch_ret": nrm(ks[24], (DEPTH, RET_WIDTH, D_MODEL), RET_WIDTH ** -0.5),
        "w_gate": nrm(ks[25], (DEPTH, D_MODEL, N_BRANCH * D_MODEL), D_MODEL ** -0.5),
        "b_gate": nrm(ks[26], (DEPTH, N_BRANCH * D_MODEL), 0.01),
        "w_o": nrm(ks[27], (DEPTH, D_MODEL, D_MODEL), D_MODEL ** -0.5),
        "norm_ffn": gain(ks[28], (DEPTH, D_MODEL)),
        "w_ff_up": nrm(ks[29], (DEPTH, D_MODEL, D_FF), D_MODEL ** -0.5),
        "w_ff_down": nrm(ks[30], (DEPTH, D_FF, D_MODEL), D_FF ** -0.5),
    }


def reference(x_prompt, x_sample, cache_k, cache_v, state_ssm_re, state_ssm_im, state_ret, page_table,
              norm_mix, w_in, sb_q_norm, sb_k_norm, sb_logit_bias, ssm_lambda_re, ssm_lambda_im, ssm_log_dt,
              ssm_b_re, ssm_b_im, ssm_c_re, ssm_c_im, ssm_d, ssm_w_glu, ret_norm,
              w_branch_ssm, w_branch_sb, w_branch_ret, w_gate, b_gate, w_o,
              norm_ffn, w_ff_up, w_ff_down):
    f32 = jnp.float32
    n_prompt = x_prompt.shape[0]
    n_dec, n_pages = page_table.shape
    yp, ys = x_prompt, x_sample
    kp, vp, srp, sip, rp = [], [], [], [], []
    kq, vq, srs, sis, rs = [], [], [], [], []
    for l in range(DEPTH):
        lw = (norm_mix[l], w_in[l], sb_q_norm[l], sb_k_norm[l], sb_logit_bias[l], ssm_lambda_re[l],
              ssm_lambda_im[l], ssm_log_dt[l], ssm_b_re[l], ssm_b_im[l], ssm_c_re[l], ssm_c_im[l], ssm_d[l],
              ssm_w_glu[l], ret_norm[l], w_branch_ssm[l], w_branch_sb[l], w_branch_ret[l], w_gate[l], b_gate[l],
              w_o[l], norm_ffn[l], w_ff_up[l], w_ff_down[l])
        zeros_ssm = jnp.zeros((n_prompt, SSM_GROUPS, SSM_STATE), f32)
        zeros_ret = jnp.zeros((n_prompt, RET_HEADS, RET_HEAD_DIM, RET_HEAD_DIM), f32)
        yp, k_new, v_new, sr, si, rt = trunk_layer(yp, None, None, zeros_ssm, zeros_ssm, zeros_ret, *lw)
        kp.append(k_new); vp.append(v_new); srp.append(sr); sip.append(si); rp.append(rt)
        past_k = cache_k[l][page_table].reshape(n_dec, n_pages * PAGE_SIZE, SB_HEADS, SB_HEAD_DIM)
        past_v = cache_v[l][page_table].reshape(n_dec, n_pages * PAGE_SIZE, SB_HEADS, SB_HEAD_DIM)
        ys, k_new, v_new, sr, si, rt = trunk_layer(ys, past_k, past_v, state_ssm_re[l], state_ssm_im[l],
                                                   state_ret[l], *lw)
        kq.append(k_new); vq.append(v_new); srs.append(sr); sis.append(si); rs.append(rt)
    return (yp, ys,
            jnp.stack(kp), jnp.stack(vp), jnp.stack(srp), jnp.stack(sip), jnp.stack(rp),
            jnp.stack(kq), jnp.stack(vq), jnp.stack(srs), jnp.stack(sis), jnp.stack(rs))
```

```python
import functools

import jax
import jax.numpy as jnp
from jax import lax
from jax.experimental import pallas as pl
from jax.experimental.pallas import tpu as pltpu

F32 = jnp.float32
BF16 = jnp.bfloat16

EPS = 1e-6
D_MODEL = 1024
SSM_GROUP = 16
SSM_GROUPS = 32
SSM_STATE = 64
SSM_CHUNK = 16
SB_HEAD_DIM = 64
SB_HEADS = 8
RET_HEAD_DIM = 128
RET_HEADS = 4
RET_CHUNK = 128
ROPE_BASE = 10000.0
D_FF = 4 * D_MODEL
PAGE_SIZE = 128
BLK = 512
IN_WIDTH = 8 * BLK
GATE_WIDTH = 3 * D_MODEL
CAT_WIDTH = IN_WIDTH + GATE_WIDTH
LANES = 128
SB_TILE = 256
PAGES_PER_STEP = 8
VMEM_LIMIT = 56 * 1024 * 1024

_NT = (((1,), (1,)), ((), ()))
_TN = (((0,), (0,)), ((), ()))


def _dot(a, b):
    return jnp.dot(a, b, preferred_element_type=F32)


def _dot_nt(a, b):
    return lax.dot_general(a, b, _NT, preferred_element_type=F32)


def _split2(x):
    hi = x.astype(BF16)
    lo = (x - hi.astype(F32)).astype(BF16)
    return hi, lo


def _dot_x3(a, b, nt=False):
    d = _dot_nt if nt else _dot
    ah, al = _split2(a)
    bh, bl = _split2(b)
    return d(ah, bh) + (d(ah, bl) + d(al, bh))


def _div_pow2(x, d):
    assert d & (d - 1) == 0
    return lax.shift_right_logical(x, jnp.int32(d.bit_length() - 1))


def _softplus(z):
    return jnp.maximum(z, 0.0) + jnp.log1p(jnp.exp(-jnp.abs(z)))


def _params(sem):
    return pltpu.CompilerParams(dimension_semantics=sem, vmem_limit_bytes=VMEM_LIMIT)


def _in_proj_kernel(x_ref, gmix_ref, w_ref, bias_ref, qg_ref, kg_ref, bd_ref,
                    big_ref, k_ref, v_ref, h_scr):
    j = pl.program_id(1)

    @pl.when(j == 0)
    def _():
        x = x_ref[...]
        ms = jnp.mean(x * x, axis=-1, keepdims=True)
        h_scr[...] = (x * lax.rsqrt(ms + EPS) * gmix_ref[...]).astype(BF16)

    acc = _dot(h_scr[...], w_ref[...])

    def head_norm(a, g_ref):
        hi, lo = _split2(a * a)
        msq = _dot(hi, bd_ref[...]) + _dot(lo, bd_ref[...])
        return a * lax.rsqrt(msq + EPS) * g_ref[...]

    @pl.when(j == 1)
    def _():
        big_ref[...] = head_norm(acc, qg_ref)

    @pl.when(j == 2)
    def _():
        kn = head_norm(acc, kg_ref)
        big_ref[...] = kn
        k_ref[...] = kn

    @pl.when(j == 3)
    def _():
        big_ref[...] = acc
        v_ref[...] = acc

    @pl.when(jnp.logical_or(j == 0, jnp.logical_and(j >= 4, j < 8)))
    def _():
        big_ref[...] = acc

    @pl.when(j >= 8)
    def _():
        big_ref[...] = jax.nn.sigmoid(acc + bias_ref[...])


def _in_proj(x2d, gmix, wcat, bias_cat, qg, kg, bd, tm):
    m = x2d.shape[0]
    sds = jax.ShapeDtypeStruct
    return pl.pallas_call(
        _in_proj_kernel,
        grid=(m // tm, CAT_WIDTH // BLK),
        in_specs=[
            pl.BlockSpec((tm, D_MODEL), lambda i, j: (i, 0)),
            pl.BlockSpec((1, D_MODEL), lambda i, j: (0, 0)),
            pl.BlockSpec((D_MODEL, BLK), lambda i, j: (0, j)),
            pl.BlockSpec((1, BLK), lambda i, j: (0, j)),
            pl.BlockSpec((1, BLK), lambda i, j: (0, 0)),
            pl.BlockSpec((1, BLK), lambda i, j: (0, 0)),
            pl.BlockSpec((BLK, BLK), lambda i, j: (0, 0)),
        ],
        out_specs=[
            pl.BlockSpec((tm, BLK), lambda i, j: (i, j)),
            pl.BlockSpec((tm, BLK), lambda i, j: (i, 0)),
            pl.BlockSpec((tm, BLK), lambda i, j: (i, 0)),
        ],
        out_shape=[sds((m, CAT_WIDTH), F32), sds((m, BLK), F32), sds((m, BLK), F32)],
        scratch_shapes=[pltpu.VMEM((tm, D_MODEL), BF16)],
        compiler_params=_params(("parallel", "arbitrary")),
        name="in_proj",
    )(x2d, gmix, wcat, bias_cat, qg, kg, bd)


def _ssm_prep_kernel(lre_ref, lim_ref, ldt_ref, btre_ref, btim_ref, cre_ref, cim_ref,
                     toep_ref, bpre_ref, bpim_ref, cpre_ref, cpim_ref, lam_ref):
    t_len = SSM_CHUNK
    lre = lre_ref[...]
    lim = lim_ref[...]
    dt = jnp.exp(ldt_ref[...])
    a = lre * dt
    b = lim * dt
    ea = jnp.exp(a)
    lbr = ea * jnp.cos(b)
    lbi = ea * jnp.sin(b)
    den = lre * lre + lim * lim
    xr = lbr - 1.0
    cr = (xr * lre + lbi * lim) / den
    ci = (lbi * lre - xr * lim) / den
    btre = btre_ref[...]
    btim = btim_ref[...]
    bbr = cr * btre - ci * btim
    bbi = cr * btim + ci * btre
    cre = cre_ref[...]
    cim = cim_ref[...]

    one = jnp.ones_like(lre)
    zero = jnp.zeros_like(lre)
    pows = [(one, zero)]
    for _ in range(t_len):
        pr, pi = pows[-1]
        pows.append((pr * lbr - pi * lbi, pr * lbi + pi * lbr))

    def cmul(xre, xim, p):
        return xre * p[0] - xim * p[1], xre * p[1] + xim * p[0]

    def stack(parts):
        return jnp.concatenate(parts, axis=0)

    a_parts = [cmul(cre, cim, pows[t]) for t in range(t_len)]
    a1_parts = [cmul(cre, cim, pows[t + 1]) for t in range(t_len)]
    h_parts = [cmul(bbr, bbi, pows[t_len - 1 - s]) for s in range(t_len)]
    a_re = stack([p[0] for p in a_parts])
    a_im = stack([p[1] for p in a_parts])

    tn = t_len * SSM_GROUP
    r0 = _dot_x3(bbr, a_re, nt=True) - _dot_x3(bbi, a_im, nt=True)
    lane = lax.broadcasted_iota(jnp.int32, (SSM_GROUP, tn), 1)
    blocks = [r0]
    for s in range(1, t_len):
        blocks.append(jnp.where(lane >= s * SSM_GROUP, pltpu.roll(r0, s * SSM_GROUP, 1), 0.0))
    toep_ref[...] = stack(blocks)
    bpre_ref[...] = stack([p[0] for p in h_parts])
    bpim_ref[...] = stack([p[1] for p in h_parts])
    cpre_ref[...] = stack([p[0] for p in a1_parts])
    cpim_ref[...] = -stack([p[1] for p in a1_parts])
    half = t_len // 2
    lam_ref[...] = stack([pows[t_len][0], pows[t_len][1], pows[half][0], pows[half][1],
                          zero, zero, zero, zero])


def _ssm_prep(lre, lim, ldt, btre, btim, cre, cim):
    g = lre.shape[0]
    n, p = SSM_GROUP, SSM_STATE
    tn = SSM_CHUNK * n
    sds = jax.ShapeDtypeStruct

    def spec(r, c):
        return pl.BlockSpec((None, r, c), lambda i: (i, 0, 0))

    return pl.pallas_call(
        _ssm_prep_kernel,
        grid=(g,),
        in_specs=[spec(1, p), spec(1, p), spec(1, 1), spec(n, p), spec(n, p), spec(n, p), spec(n, p)],
        out_specs=[spec(tn, tn), spec(tn, p), spec(tn, p), spec(tn, p), spec(tn, p), spec(8, p)],
        out_shape=[sds((g, tn, tn), F32), sds((g, tn, p), F32), sds((g, tn, p), F32),
                   sds((g, tn, p), F32), sds((g, tn, p), F32), sds((g, 8, p), F32)],
        compiler_params=_params(("parallel",)),
        name="ssm_prep",
    )(lre, lim, ldt, btre, btim, cre, cim)


def _ssm_kernel(u_ref, x0_ref, toep_ref, bp_ref, cp_ref, lam_ref, d_ref, y_ref, xf_ref,
                w_scr, xall_scr, *, rows_per_chunk, n_chunks):
    r = rows_per_chunk
    u = u_ref[...]
    w_scr[...] = _dot_x3(u, bp_ref[...])
    l1 = lam_ref[0:1, :]
    l2 = lam_ref[1:2, :]

    def body(c, x):
        r0 = pl.multiple_of(c * r, r)
        xall_scr[pl.ds(r0, r), :] = x
        return x * l1 + pltpu.roll(x, SSM_STATE, 1) * l2 + w_scr[pl.ds(r0, r), :]

    xf_ref[...] = lax.fori_loop(0, n_chunks, body, x0_ref[...])
    y = _dot_x3(u, toep_ref[...]) + _dot_x3(xall_scr[...], cp_ref[...], nt=True) + u * d_ref[...]
    y_ref[...] = jax.nn.gelu(y)


def _ssm(u_g, x0_g, toep, bp, cp, lam, dfull, rows_per_chunk):
    g, rows, tn = u_g.shape
    p2 = 2 * SSM_STATE
    sds = jax.ShapeDtypeStruct

    def spec(r, c):
        return pl.BlockSpec((None, r, c), lambda i: (i, 0, 0))

    kern = functools.partial(_ssm_kernel, rows_per_chunk=rows_per_chunk, n_chunks=rows // rows_per_chunk)
    return pl.pallas_call(
        kern,
        grid=(g,),
        in_specs=[spec(rows, tn), spec(rows_per_chunk, p2), spec(tn, tn), spec(tn, p2), spec(tn, p2),
                  spec(8, p2), spec(1, tn)],
        out_specs=[spec(rows, tn), spec(rows_per_chunk, p2)],
        out_shape=[sds((g, rows, tn), F32), sds((g, rows_per_chunk, p2), F32)],
        scratch_shapes=[pltpu.VMEM((rows, p2), F32), pltpu.VMEM((rows, p2), F32)],
        compiler_params=_params(("parallel",)),
        name="ssm",
    )(u_g, x0_g, toep, bp, cp, lam, dfull)


def _sb_block(z, carry, to, v_bf, mask):
    keys = z.shape[1]
    sp = _softplus(z)
    ls = -sp
    if mask is not None:
        ls = jnp.where(mask, ls, 0.0)
    hi, lo = _split2(ls)
    ct = _dot(hi, to) + _dot(lo, to)
    carry_b = carry if keys == LANES else jnp.concatenate([carry] * (keys // LANES), axis=1)
    att = jnp.exp((z - sp) + ct[:, :keys] + carry_b)
    if mask is not None:
        att = jnp.where(mask, att, 0.0)
    return _dot(att.astype(BF16), v_bf), ct[:, keys:]


def _sb_prompt_kernel(bias_ref, q_ref, k_ref, v_ref, to_ref, o_ref, carry_scr, acc_scr, *, tile):
    hp = pl.program_id(1)
    qi = pl.program_id(2)
    q = q_ref[...] * (SB_HEAD_DIM ** -0.5)
    lane = lax.broadcasted_iota(jnp.int32, (1, LANES), 1)
    row = lax.broadcasted_iota(jnp.int32, (tile, tile), 0)
    col = lax.broadcasted_iota(jnp.int32, (tile, tile), 1)
    causal = col < row
    to = to_ref[...]
    acc_scr[...] = jnp.zeros_like(acc_scr)
    for hh in range(2):
        head_lanes = (lane < SB_HEAD_DIM) if hh == 0 else (lane >= SB_HEAD_DIM)
        qh = jnp.where(head_lanes, q, 0.0).astype(BF16)
        bias = bias_ref[hp * 2 + hh]
        carry_scr[...] = jnp.zeros_like(carry_scr)

        def block(kj, mask):
            k0 = pl.multiple_of(kj * tile, tile)
            kb = k_ref[pl.ds(k0, tile), :].astype(BF16)
            vb = jnp.where(head_lanes, v_ref[pl.ds(k0, tile), :], 0.0).astype(BF16)
            z = _dot_nt(qh, kb) + bias
            out, tot = _sb_block(z, carry_scr[...], to, vb, mask)
            acc_scr[...] += out
            carry_scr[...] += tot

        block(qi, causal)

        def body(t, _):
            block(qi - 1 - t, None)
            return 0

        lax.fori_loop(0, qi, body, 0)
    o_ref[...] = acc_scr[...]


def _sb_prompt(bias, big, k2d, v2d, to, bsz, seq):
    tile = min(SB_TILE, seq)
    nq = seq // tile
    m = bsz * seq
    kern = functools.partial(_sb_prompt_kernel, tile=tile)
    q_col0 = BLK // LANES
    return pl.pallas_call(
        kern,
        grid=(bsz, SB_HEADS // 2, nq),
        in_specs=[
            pl.BlockSpec(memory_space=pltpu.SMEM),
            pl.BlockSpec((tile, LANES), lambda b, h, i: (b * nq + i, q_col0 + h)),
            pl.BlockSpec((seq, LANES), lambda b, h, i: (b, h)),
            pl.BlockSpec((seq, LANES), lambda b, h, i: (b, h)),
            pl.BlockSpec((tile, tile + LANES), lambda b, h, i: (0, 0)),
        ],
        out_specs=pl.BlockSpec((tile, LANES), lambda b, h, i: (b * nq + i, h)),
        out_shape=jax.ShapeDtypeStruct((m, BLK), F32),
        scratch_shapes=[pltpu.VMEM((tile, LANES), F32), pltpu.VMEM((tile, LANES), F32)],
        compiler_params=_params(("parallel", "parallel", "arbitrary")),
        name="sb_prompt",
    )(bias, big, k2d, v2d, to)


def _sb_paged_kernel(pt_ref, bias_ref, q_ref, kn_ref, vn_ref, to_ref, *rest, n_q):
    del pt_ref
    npg = PAGES_PER_STEP
    k_refs = rest[:npg]
    v_refs = rest[npg:2 * npg]
    o_ref, qbd_scr, bias_scr, carry_scr, acc_scr = rest[2 * npg:]
    rows = SB_HEADS * n_q
    s = pl.program_id(1)

    def block(k_ref, v_ref, mask):
        z = _dot_nt(qbd_scr[...], k_ref[...].astype(BF16)) + bias_scr[...]
        out, tot = _sb_block(z, carry_scr[...], to_ref[...], v_ref[...].astype(BF16), mask)
        acc_scr[...] += out
        carry_scr[...] += tot

    @pl.when(s == 0)
    def _():
        row_head = _div_pow2(lax.broadcasted_iota(jnp.int32, (rows, BLK), 0), n_q)
        lane_head = _div_pow2(lax.broadcasted_iota(jnp.int32, (rows, BLK), 1), SB_HEAD_DIM)
        q_all = jnp.concatenate([q_ref[...]] * SB_HEADS, axis=0) * (SB_HEAD_DIM ** -0.5)
        qbd_scr[...] = jnp.where(row_head == lane_head, q_all, 0.0).astype(BF16)
        rh = _div_pow2(lax.broadcasted_iota(jnp.int32, (rows, LANES), 0), n_q)
        bt = jnp.zeros((rows, LANES), F32)
        for h in range(SB_HEADS):
            bt = jnp.where(rh == h, bias_ref[h], bt)
        bias_scr[...] = bt
        carry_scr[...] = jnp.zeros_like(carry_scr)
        acc_scr[...] = jnp.zeros_like(acc_scr)
        qpos = lax.broadcasted_iota(jnp.int32, (rows, PAGE_SIZE), 0) & (n_q - 1)
        kpos = lax.broadcasted_iota(jnp.int32, (rows, PAGE_SIZE), 1)
        block(kn_ref, vn_ref, kpos < qpos)

    for i in range(npg):
        block(k_refs[i], v_refs[i], None)

    @pl.when(s == pl.num_programs(1) - 1)
    def _():
        lane_head = _div_pow2(lax.broadcasted_iota(jnp.int32, (n_q, BLK), 1), SB_HEAD_DIM)
        out = jnp.zeros((n_q, BLK), F32)
        for h in range(SB_HEADS):
            out = jnp.where(lane_head == h, acc_scr[h * n_q:(h + 1) * n_q, :], out)
        o_ref[...] = out


def _sb_paged(page_table, bias, big, kn_pad, vn_pad, to, cache_k4, cache_v4, layer, n_q):
    n_seq, n_pages = page_table.shape
    npg = PAGES_PER_STEP
    steps = n_pages // npg
    rows = SB_HEADS * n_q
    q_col = 1

    def page_spec(i):
        def imap(b, s, pt):
            return (layer, pt[b, n_pages - 1 - (s * npg + i)], 0, 0)
        return pl.BlockSpec((None, None, PAGE_SIZE, BLK), imap)

    grid_spec = pltpu.PrefetchScalarGridSpec(
        num_scalar_prefetch=1,
        grid=(n_seq, steps),
        in_specs=[
            pl.BlockSpec(memory_space=pltpu.SMEM),
            pl.BlockSpec((n_q, BLK), lambda b, s, pt: (b, q_col)),
            pl.BlockSpec((PAGE_SIZE, BLK), lambda b, s, pt: (b, 0)),
            pl.BlockSpec((PAGE_SIZE, BLK), lambda b, s, pt: (b, 0)),
            pl.BlockSpec((PAGE_SIZE, PAGE_SIZE + LANES), lambda b, s, pt: (0, 0)),
        ] + [page_spec(i) for i in range(npg)] + [page_spec(i) for i in range(npg)],
        out_specs=pl.BlockSpec((n_q, BLK), lambda b, s, pt: (b, 0)),
        scratch_shapes=[pltpu.VMEM((rows, BLK), BF16), pltpu.VMEM((rows, LANES), F32),
                        pltpu.VMEM((rows, LANES), F32), pltpu.VMEM((rows, BLK), F32)],
    )
    kern = functools.partial(_sb_paged_kernel, n_q=n_q)
    return pl.pallas_call(
        kern,
        grid_spec=grid_spec,
        out_shape=jax.ShapeDtypeStruct((n_seq * n_q, BLK), F32),
        compiler_params=_params(("parallel", "arbitrary")),
        name="sb_paged",
    )(page_table, bias, big, kn_pad, vn_pad, to, *([cache_k4] * npg), *([cache_v4] * npg))


def _ret_kernel(q_ref, k_ref, v_ref, g_ref, cos_ref, sin_ref, dmask_ref, xi_ref, zeta_ref, gch_ref,
                s0_ref, gain_ref, o_ref, sout_ref, s_scr):
    c = pl.program_id(1)

    @pl.when(c == 0)
    def _():
        s_scr[...] = s0_ref[...]

    cos = cos_ref[...]
    sin = sin_ref[...]
    half = RET_HEAD_DIM // 2
    for h in range(RET_HEADS):
        sl = slice(h * RET_HEAD_DIM, (h + 1) * RET_HEAD_DIM)
        qh = q_ref[:, sl]
        kh = k_ref[:, sl]
        rq = qh * cos + pltpu.roll(qh, half, 1) * sin
        rk = (kh * cos + pltpu.roll(kh, half, 1) * sin) * (RET_HEAD_DIM ** -0.5)
        rq_b = rq.astype(BF16)
        rk_b = rk.astype(BF16)
        v_b = v_ref[:, sl].astype(BF16)
        state = s_scr[h]
        inner = _dot_nt(rq_b, rk_b) * dmask_ref[h]
        o = _dot(inner.astype(BF16), v_b) + _dot(rq_b, state.astype(BF16)) * xi_ref[h]
        kz = (rk * zeta_ref[h]).astype(BF16)
        s_scr[h] = state * gch_ref[h] + lax.dot_general(kz, v_b, _TN, preferred_element_type=F32)
        mu = jnp.mean(o, axis=-1, keepdims=True)
        d = o - mu
        var = jnp.mean(d * d, axis=-1, keepdims=True)
        y = d * lax.rsqrt(var + EPS) * gain_ref[:, sl]
        gate = g_ref[:, sl]
        o_ref[:, sl] = gate * jax.nn.sigmoid(gate) * y

    @pl.when(c == pl.num_programs(1) - 1)
    def _():
        sout_ref[...] = s_scr[...]


def _retention(big, cosf, sinf, dmask, xi, zeta, gch, s0, gain, bsz, seq):
    chunk = min(RET_CHUNK, seq)
    nc = seq // chunk
    m = bsz * seq
    hd = RET_HEAD_DIM
    sds = jax.ShapeDtypeStruct

    def col(j):
        return pl.BlockSpec((chunk, BLK), lambda b, c: (b * nc + c, j))

    def const3(shape):
        return pl.BlockSpec(shape, lambda b, c: (0, 0, 0))

    state_spec = pl.BlockSpec((None, RET_HEADS, hd, hd), lambda b, c: (b, 0, 0, 0))
    return pl.pallas_call(
        _ret_kernel,
        grid=(bsz, nc),
        in_specs=[
            col(4), col(5), col(6), col(7),
            pl.BlockSpec((chunk, hd), lambda b, c: (c, 0)),
            pl.BlockSpec((chunk, hd), lambda b, c: (c, 0)),
            const3((RET_HEADS, chunk, chunk)),
            const3((RET_HEADS, chunk, hd)),
            const3((RET_HEADS, chunk, hd)),
            const3((RET_HEADS, 1, hd)),
            state_spec,
            pl.BlockSpec((1, BLK), lambda b, c: (0, 0)),
        ],
        out_specs=[pl.BlockSpec((chunk, BLK), lambda b, c: (b * nc + c, 0)), state_spec],
        out_shape=[sds((m, BLK), F32), sds((bsz, RET_HEADS, hd, hd), F32)],
        scratch_shapes=[pltpu.VMEM((RET_HEADS, hd, hd), F32)],
        compiler_params=_params(("parallel", "arbitrary")),
        name="retention",
    )(big, big, big, big, cosf, sinf, dmask, xi, zeta, gch, s0, gain)


def _merge_kernel(x_ref, ga_ref, gb_ref, gc_ref, ys_ref, ob_ref, oc_ref,
                  wglu_ref, wbs_ref, wbb_ref, wbr_ref, wo_ref, o_ref):
    glu = _dot(ys_ref[...].astype(BF16), wglu_ref[...])
    out_a = glu[:, :BLK] * jax.nn.sigmoid(glu[:, BLK:])
    merged = (ga_ref[...] * _dot(out_a.astype(BF16), wbs_ref[...])
              + gb_ref[...] * _dot(ob_ref[...].astype(BF16), wbb_ref[...])
              + gc_ref[...] * _dot(oc_ref[...].astype(BF16), wbr_ref[...]))
    o_ref[...] = x_ref[...] + _dot(merged.astype(BF16), wo_ref[...])


def _merge(x2d, big, ys, ob, oc, wglu, wbs, wbb, wbr, wo, tm):
    m = x2d.shape[0]
    gate0 = IN_WIDTH // D_MODEL

    def rows(width, j=0):
        return pl.BlockSpec((tm, width), lambda i: (i, j))

    def whole(shape):
        return pl.BlockSpec(shape, lambda i: (0, 0))

    return pl.pallas_call(
        _merge_kernel,
        grid=(m // tm,),
        in_specs=[rows(D_MODEL), rows(D_MODEL, gate0), rows(D_MODEL, gate0 + 1), rows(D_MODEL, gate0 + 2),
                  rows(BLK), rows(BLK), rows(BLK),
                  whole((BLK, 2 * BLK)), whole((BLK, D_MODEL)), whole((BLK, D_MODEL)), whole((BLK, D_MODEL)),
                  whole((D_MODEL, D_MODEL))],
        out_specs=rows(D_MODEL),
        out_shape=jax.ShapeDtypeStruct((m, D_MODEL), F32),
        compiler_params=_params(("parallel",)),
        name="merge",
    )(x2d, big, big, big, ys, ob, oc, wglu, wbs, wbb, wbr, wo)


def _ffn_kernel(x_ref, g_ref, wup_ref, wdn_ref, o_ref, h_scr, acc_scr):
    f = pl.program_id(1)

    @pl.when(f == 0)
    def _():
        x = x_ref[...]
        ms = jnp.mean(x * x, axis=-1, keepdims=True)
        h_scr[...] = (x * lax.rsqrt(ms + EPS) * g_ref[...]).astype(BF16)
        acc_scr[...] = x

    up = jnp.maximum(_dot(h_scr[...], wup_ref[...]), 0.0)
    acc_scr[...] += _dot((up * up).astype(BF16), wdn_ref[...])

    @pl.when(f == pl.num_programs(1) - 1)
    def _():
        o_ref[...] = acc_scr[...]


def _ffn(x2d, g, wup, wdn, tm, tf):
    m = x2d.shape[0]
    return pl.pallas_call(
        _ffn_kernel,
        grid=(m // tm, D_FF // tf),
        in_specs=[
            pl.BlockSpec((tm, D_MODEL), lambda i, f: (i, 0)),
            pl.BlockSpec((1, D_MODEL), lambda i, f: (0, 0)),
            pl.BlockSpec((D_MODEL, tf), lambda i, f: (0, f)),
            pl.BlockSpec((tf, D_MODEL), lambda i, f: (f, 0)),
        ],
        out_specs=pl.BlockSpec((tm, D_MODEL), lambda i, f: (i, 0)),
        out_shape=jax.ShapeDtypeStruct((m, D_MODEL), F32),
        scratch_shapes=[pltpu.VMEM((tm, D_MODEL), BF16), pltpu.VMEM((tm, D_MODEL), F32)],
        compiler_params=_params(("parallel", "arbitrary")),
        name="ffn",
    )(x2d, g, wup, wdn)


def _tri_ones(keys):
    j = jnp.arange(keys)[:, None]
    s = jnp.arange(keys)[None, :]
    return jnp.concatenate([(j > s).astype(BF16), jnp.ones((keys, LANES), BF16)], axis=1)


def _rotary_tables(start, seq):
    half = RET_HEAD_DIM // 2
    pos = start + jnp.arange(seq, dtype=jnp.int32)
    inv_freq = ROPE_BASE ** (-jnp.arange(half, dtype=F32) / half)
    ang = pos.astype(F32)[:, None] * inv_freq[None, :]
    cos = jnp.cos(ang)
    sin = jnp.sin(ang)
    return jnp.concatenate([cos, cos], axis=1), jnp.concatenate([-sin, sin], axis=1)


def _decay_tables(chunk):
    log_g = jnp.log1p(-jnp.exp2(-5.0 - jnp.arange(RET_HEADS, dtype=F32)))
    i = jnp.arange(chunk, dtype=F32)
    diff = i[:, None] - i[None, :]
    dmask = jnp.where(diff[None] >= 0, jnp.exp(jnp.maximum(diff, 0.0)[None] * log_g[:, None, None]), 0.0)
    xi = jnp.exp((i[None, :] + 1.0) * log_g[:, None])
    zeta = jnp.exp((chunk - 1.0 - i)[None, :] * log_g[:, None])
    gch = jnp.exp(chunk * log_g)
    hd = RET_HEAD_DIM
    bc = lambda t: jnp.broadcast_to(t[:, :, None], (RET_HEADS, chunk, hd))
    return dmask, bc(xi), bc(zeta), jnp.broadcast_to(gch[:, None, None], (RET_HEADS, 1, hd))


def _ssm_tables(prep, t_len):
    toep, bpre, bpim, cpre, cpim, lam = prep
    tn = t_len * SSM_GROUP
    off = (SSM_CHUNK - t_len) * SSM_GROUP
    bp = jnp.concatenate([bpre, bpim], axis=-1)[:, off:, :]
    cp = jnp.concatenate([cpre, cpim], axis=-1)[:, :tn, :]
    r = 0 if t_len == SSM_CHUNK else 2
    lr, li = lam[:, r:r + 1, :], lam[:, r + 1:r + 2, :]
    rows = jnp.concatenate([jnp.concatenate([lr, lr], -1), jnp.concatenate([-li, li], -1)], axis=1)
    lam8 = jnp.concatenate([rows, jnp.zeros((rows.shape[0], 6, rows.shape[2]), F32)], axis=1)
    return toep[:, :tn, :tn], bp, cp, lam8


def _layer(x3d, past, x0_re, x0_im, s0, lw, prep, consts, layer):
    bsz, seq, _ = x3d.shape
    m = bsz * seq
    x2d = x3d.reshape(m, D_MODEL)
    tm_big = min(1024, m)

    big, k2d, v2d = _in_proj(x2d, lw["gmix"], lw["wcat"], lw["bias_cat"], lw["qg"], lw["kg"], consts["bd"], tm_big)

    t_len = min(SSM_CHUNK, seq)
    nchunk = seq // t_len
    g, n = SSM_GROUPS, SSM_GROUP
    u_g = big[:, :BLK].reshape(bsz, nchunk, t_len, g, n).transpose(3, 1, 0, 2, 4).reshape(g, nchunk * bsz, t_len * n)
    x0_g = jnp.concatenate([x0_re, x0_im], axis=-1).transpose(1, 0, 2)
    toep, bp, cp, lam8 = _ssm_tables(prep, t_len)
    dfull = jnp.tile(lw["ssm_d"], (1, t_len))[:, None, :]
    y_g, xf_g = _ssm(u_g, x0_g, toep, bp, cp, lam8, dfull, bsz)
    ys = y_g.reshape(g, nchunk, bsz, t_len, n).transpose(2, 1, 3, 0, 4).reshape(m, BLK)
    xf = xf_g.transpose(1, 0, 2)
    ssm_re, ssm_im = xf[..., :SSM_STATE], xf[..., SSM_STATE:]

    if past is None:
        ob = _sb_prompt(lw["sb_bias"], big, k2d, v2d, consts["to_prompt"](min(SB_TILE, seq)), bsz, seq)
        start = 0
    else:
        cache_k4, cache_v4, page_table = past
        pad = lambda t: jnp.pad(t.reshape(bsz, seq, BLK), ((0, 0), (0, PAGE_SIZE - seq), (0, 0))).reshape(bsz * PAGE_SIZE, BLK)
        ob = _sb_paged(page_table, lw["sb_bias"], big, pad(k2d), pad(v2d), consts["to_page"],
                       cache_k4, cache_v4, layer, seq)
        start = page_table.shape[1] * PAGE_SIZE

    cosf, sinf = _rotary_tables(start, seq)
    dmask, xi, zeta, gch = _decay_tables(min(RET_CHUNK, seq))
    oc, ret_state = _retention(big, cosf, sinf, dmask, xi, zeta, gch, s0, lw["ret_gain"], bsz, seq)

    x1 = _merge(x2d, big, ys, ob, oc, lw["wglu"], lw["wbs"], lw["wbb"], lw["wbr"], lw["wo"], min(256, m))
    x2 = _ffn(x1, lw["gffn"], lw["wup"], lw["wdn"], tm_big, 1024)
    shape_kv = (bsz, seq, SB_HEADS, SB_HEAD_DIM)
    return x2.reshape(bsz, seq, D_MODEL), k2d.reshape(shape_kv), v2d.reshape(shape_kv), ssm_re, ssm_im, ret_state


def kernel(x_prompt, x_sample, cache_k, cache_v, state_ssm_re, state_ssm_im, state_ret, page_table, norm_mix, w_in, sb_q_norm, sb_k_norm, sb_logit_bias, ssm_lambda_re, ssm_lambda_im, ssm_log_dt, ssm_b_re, ssm_b_im, ssm_c_re, ssm_c_im, ssm_d, ssm_w_glu, ret_norm, w_branch_ssm, w_branch_sb, w_branch_ret, w_gate, b_gate, w_o, norm_ffn, w_ff_up, w_ff_down):
    depth = w_in.shape[0]
    n_prompt = x_prompt.shape[0]
    n_phys = cache_k.shape[1]
    cache_k4 = cache_k.reshape(depth, n_phys, PAGE_SIZE, BLK)
    cache_v4 = cache_v.reshape(depth, n_phys, PAGE_SIZE, BLK)
    lane_head = jnp.arange(BLK) // SB_HEAD_DIM
    consts = {
        "bd": jnp.where(lane_head[:, None] == lane_head[None, :], 1.0 / SB_HEAD_DIM, 0.0).astype(BF16),
        "to_prompt": _tri_ones,
        "to_page": _tri_ones(PAGE_SIZE),
    }
    zeros_ssm = jnp.zeros((n_prompt, SSM_GROUPS, SSM_STATE), F32)
    zeros_ret = jnp.zeros((n_prompt, RET_HEADS, RET_HEAD_DIM, RET_HEAD_DIM), F32)

    yp, ys = x_prompt, x_sample
    outs_p, outs_s = [], []
    for l in range(depth):
        lw = {
            "gmix": norm_mix[l][None, :],
            "wcat": jnp.concatenate([w_in[l], w_gate[l]], axis=1).astype(BF16),
            "bias_cat": jnp.concatenate([jnp.zeros((IN_WIDTH,), F32), b_gate[l]])[None, :],
            "qg": jnp.tile(sb_q_norm[l], SB_HEADS)[None, :],
            "kg": jnp.tile(sb_k_norm[l], SB_HEADS)[None, :],
            "sb_bias": sb_logit_bias[l],
            "ssm_d": ssm_d[l],
            "ret_gain": ret_norm[l][None, :],
            "wglu": ssm_w_glu[l].astype(BF16),
            "wbs": w_branch_ssm[l].astype(BF16),
            "wbb": w_branch_sb[l].astype(BF16),
            "wbr": w_branch_ret[l].astype(BF16),
            "wo": w_o[l].astype(BF16),
            "gffn": norm_ffn[l][None, :],
            "wup": w_ff_up[l].astype(BF16),
            "wdn": w_ff_down[l].astype(BF16),
        }
        prep = _ssm_prep(ssm_lambda_re[l][:, None, :], ssm_lambda_im[l][:, None, :], ssm_log_dt[l][:, None, None],
                         ssm_b_re[l].transpose(0, 2, 1), ssm_b_im[l].transpose(0, 2, 1), ssm_c_re[l], ssm_c_im[l])
        rp = _layer(yp, None, zeros_ssm, zeros_ssm, zeros_ret, lw, prep, consts, l)
        rs = _layer(ys, (cache_k4, cache_v4, page_table), state_ssm_re[l], state_ssm_im[l], state_ret[l],
                    lw, prep, consts, l)
        yp, ys = rp[0], rs[0]
        outs_p.append(rp[1:])
        outs_s.append(rs[1:])
    stack = lambda outs, i: jnp.stack([o[i] for o in outs])
    return (yp, ys) + tuple(stack(outs_p, i) for i in range(5)) + tuple(stack(outs_s, i) for i in range(5))
```

```python
import functools

import jax
import jax.numpy as jnp
from jax import lax
from jax.experimental import pallas as pl
from jax.experimental.pallas import tpu as pltpu

F32 = jnp.float32
BF16 = jnp.bfloat16

EPS = 1e-6
D_MODEL = 1024
SSM_GROUP = 16
SSM_GROUPS = 32
SSM_STATE = 64
SSM_CHUNK = 16
SSM_GROUPS_PER_STEP = 4
SB_HEAD_DIM = 64
SB_HEADS = 8
RET_HEAD_DIM = 128
RET_HEADS = 4
RET_CHUNK = 128
ROPE_BASE = 10000.0
D_FF = 4 * D_MODEL
PAGE_SIZE = 128
BLK = 512
IN_WIDTH = 8 * BLK
GATE_WIDTH = 3 * D_MODEL
LANES = 128
SB_TILE = 256
PAGES_PER_STEP = 8
VMEM_LIMIT = 56 * 1024 * 1024

_NT = (((1,), (1,)), ((), ()))
_TN = (((0,), (0,)), ((), ()))


def _dot(a, b):
    return jnp.dot(a, b, preferred_element_type=F32)


def _dot_nt(a, b):
    return lax.dot_general(a, b, _NT, preferred_element_type=F32)


def _split2(x):
    hi = x.astype(BF16)
    lo = (x - hi.astype(F32)).astype(BF16)
    return hi, lo


def _dot_x3(a, b, nt=False):
    d = _dot_nt if nt else _dot
    ah, al = _split2(a)
    bh, bl = _split2(b)
    return d(ah, bh) + (d(ah, bl) + d(al, bh))


def _div_pow2(x, d):
    assert d & (d - 1) == 0
    return lax.shift_right_logical(x, jnp.int32(d.bit_length() - 1))


def _softplus(z):
    return jnp.maximum(z, 0.0) + jnp.log(1.0 + jnp.exp(-jnp.abs(z)))


def _rms_norm(x, g):
    ms = jnp.mean(x * x, axis=-1, keepdims=True)
    return x * lax.rsqrt(ms + EPS) * g


def _params(sem):
    return pltpu.CompilerParams(dimension_semantics=sem, vmem_limit_bytes=VMEM_LIMIT)


def _in_proj_kernel(x_ref, gmix_ref, w_ref, qg_ref, kg_ref, bd_ref, big_ref, k_ref, v_ref, h_scr):
    j = pl.program_id(1)

    @pl.when(j == 0)
    def _():
        h_scr[...] = _rms_norm(x_ref[...], gmix_ref[...]).astype(BF16)

    acc = _dot(h_scr[...], w_ref[...])

    def head_norm(a, g_ref):
        hi, lo = _split2(a * a)
        msq = _dot(hi, bd_ref[...]) + _dot(lo, bd_ref[...])
        return a * lax.rsqrt(msq + EPS) * g_ref[...]

    @pl.when(j == 1)
    def _():
        big_ref[...] = head_norm(acc, qg_ref)

    @pl.when(j == 2)
    def _():
        kn = head_norm(acc, kg_ref)
        big_ref[...] = kn
        k_ref[...] = kn

    @pl.when(j == 3)
    def _():
        big_ref[...] = acc
        v_ref[...] = acc

    @pl.when(jnp.logical_or(j == 0, j >= 4))
    def _():
        big_ref[...] = acc


def _in_proj(x2d, gmix, w_in, qg, kg, bd, tm):
    m = x2d.shape[0]
    sds = jax.ShapeDtypeStruct
    return pl.pallas_call(
        _in_proj_kernel,
        grid=(m // tm, IN_WIDTH // BLK),
        in_specs=[
            pl.BlockSpec((tm, D_MODEL), lambda i, j: (i, 0)),
            pl.BlockSpec((1, D_MODEL), lambda i, j: (0, 0)),
            pl.BlockSpec((D_MODEL, BLK), lambda i, j: (0, j)),
            pl.BlockSpec((1, BLK), lambda i, j: (0, 0)),
            pl.BlockSpec((1, BLK), lambda i, j: (0, 0)),
            pl.BlockSpec((BLK, BLK), lambda i, j: (0, 0)),
        ],
        out_specs=[
            pl.BlockSpec((tm, BLK), lambda i, j: (i, j)),
            pl.BlockSpec((tm, BLK), lambda i, j: (i, 0)),
            pl.BlockSpec((tm, BLK), lambda i, j: (i, 0)),
        ],
        out_shape=[sds((m, IN_WIDTH), F32), sds((m, BLK), F32), sds((m, BLK), F32)],
        scratch_shapes=[pltpu.VMEM((tm, D_MODEL), BF16)],
        compiler_params=_params(("parallel", "arbitrary")),
        name="in_proj",
    )(x2d, gmix, w_in, qg, kg, bd)


def _ssm_prep_kernel(lre_ref, lim_ref, ldt_ref, btre_ref, btim_ref, cre_ref, cim_ref,
                     toep_ref, bpre_ref, bpim_ref, cpre_ref, cpim_ref, lam_ref):
    t_len = SSM_CHUNK
    lre = lre_ref[...]
    lim = lim_ref[...]
    dt = jnp.exp(ldt_ref[...])
    a = lre * dt
    b = lim * dt
    ea = jnp.exp(a)
    lbr = ea * jnp.cos(b)
    lbi = ea * jnp.sin(b)
    den = lre * lre + lim * lim
    xr = lbr - 1.0
    cr = (xr * lre + lbi * lim) / den
    ci = (lbi * lre - xr * lim) / den
    btre = btre_ref[...]
    btim = btim_ref[...]
    bbr = cr * btre - ci * btim
    bbi = cr * btim + ci * btre
    cre = cre_ref[...]
    cim = cim_ref[...]

    one = jnp.ones_like(lre)
    zero = jnp.zeros_like(lre)
    pows = [(one, zero)]
    for _ in range(t_len):
        pr, pi = pows[-1]
        pows.append((pr * lbr - pi * lbi, pr * lbi + pi * lbr))

    def cmul(xre, xim, p):
        return xre * p[0] - xim * p[1], xre * p[1] + xim * p[0]

    def stack(parts):
        return jnp.concatenate(parts, axis=0)

    a_parts = [cmul(cre, cim, pows[t]) for t in range(t_len)]
    a1_parts = [cmul(cre, cim, pows[t + 1]) for t in range(t_len)]
    h_parts = [cmul(bbr, bbi, pows[t_len - 1 - s]) for s in range(t_len)]
    a_re = stack([p[0] for p in a_parts])
    a_im = stack([p[1] for p in a_parts])

    tn = t_len * SSM_GROUP
    r0 = _dot_x3(bbr, a_re, nt=True) - _dot_x3(bbi, a_im, nt=True)
    lane = lax.broadcasted_iota(jnp.int32, (SSM_GROUP, tn), 1)
    blocks = [r0]
    for s in range(1, t_len):
        blocks.append(jnp.where(lane >= s * SSM_GROUP, pltpu.roll(r0, s * SSM_GROUP, 1), 0.0))
    toep_ref[...] = stack(blocks)
    bpre_ref[...] = stack([p[0] for p in h_parts])
    bpim_ref[...] = stack([p[1] for p in h_parts])
    cpre_ref[...] = stack([p[0] for p in a1_parts])
    cpim_ref[...] = -stack([p[1] for p in a1_parts])
    half = t_len // 2
    lam_ref[...] = stack([pows[t_len][0], pows[t_len][1], pows[half][0], pows[half][1],
                          zero, zero, zero, zero])


def _ssm_prep(lre, lim, ldt, btre, btim, cre, cim):
    g = lre.shape[0]
    n, p = SSM_GROUP, SSM_STATE
    tn = SSM_CHUNK * n
    sds = jax.ShapeDtypeStruct

    def spec(r, c):
        return pl.BlockSpec((None, r, c), lambda i: (i, 0, 0))

    return pl.pallas_call(
        _ssm_prep_kernel,
        grid=(g,),
        in_specs=[spec(1, p), spec(1, p), spec(1, 1), spec(n, p), spec(n, p), spec(n, p), spec(n, p)],
        out_specs=[spec(tn, tn), spec(tn, p), spec(tn, p), spec(tn, p), spec(tn, p), spec(8, p)],
        out_shape=[sds((g, tn, tn), F32), sds((g, tn, p), F32), sds((g, tn, p), F32),
                   sds((g, tn, p), F32), sds((g, tn, p), F32), sds((g, 8, p), F32)],
        compiler_params=_params(("parallel",)),
        name="ssm_prep",
    )(lre, lim, ldt, btre, btim, cre, cim)


def _ssm_kernel(u_ref, x0_ref, toep_ref, bp_ref, bps_ref, cp_ref, lam_ref, d_ref, y_ref, xf_ref,
                w_scr, ws_scr, xall_scr, *, rows_per_chunk, n_chunks):
    r = rows_per_chunk
    ng = u_ref.shape[0]
    for g in range(ng):
        u = u_ref[g]
        w_scr[g] = _dot_x3(u, bp_ref[g])
        ws_scr[g] = _dot_x3(u, bps_ref[g])

    l1 = [jnp.broadcast_to(lam_ref[g, 0:1, :], (r, 2 * SSM_STATE)) for g in range(ng)]
    l2 = [jnp.broadcast_to(lam_ref[g, 1:2, :], (r, 2 * SSM_STATE)) for g in range(ng)]

    def body(c, carry):
        r0 = pl.multiple_of(c * r, r)
        out = []
        for g in range(ng):
            x, xs = carry[2 * g], carry[2 * g + 1]
            xall_scr[g, pl.ds(r0, r), :] = x
            out.append(x * l1[g] + xs * l2[g] + w_scr[g, pl.ds(r0, r), :])
            out.append(xs * l1[g] - x * l2[g] + ws_scr[g, pl.ds(r0, r), :])
        return tuple(out)

    init = []
    for g in range(ng):
        init += [x0_ref[g, 0], x0_ref[g, 1]]
    fin = lax.fori_loop(0, n_chunks, body, tuple(init))
    for g in range(ng):
        xf_ref[g] = fin[2 * g]
        u = u_ref[g]
        y = _dot_x3(u, toep_ref[g]) + _dot_x3(xall_scr[g], cp_ref[g], nt=True) + u * d_ref[g]
        y_ref[g] = jax.nn.gelu(y)


def _ssm(u_g, x0_g, toep, bp, bps, cp, lam, dfull, rows_per_chunk):
    g, rows, tn = u_g.shape
    p2 = 2 * SSM_STATE
    gb = SSM_GROUPS_PER_STEP
    sds = jax.ShapeDtypeStruct

    def spec(*shape):
        return pl.BlockSpec((gb,) + shape, lambda i: (i,) + (0,) * len(shape))

    kern = functools.partial(_ssm_kernel, rows_per_chunk=rows_per_chunk, n_chunks=rows // rows_per_chunk)
    return pl.pallas_call(
        kern,
        grid=(g // gb,),
        in_specs=[spec(rows, tn), spec(2, rows_per_chunk, p2), spec(tn, tn), spec(tn, p2), spec(tn, p2),
                  spec(tn, p2), spec(8, p2), spec(1, tn)],
        out_specs=[spec(rows, tn), spec(rows_per_chunk, p2)],
        out_shape=[sds((g, rows, tn), F32), sds((g, rows_per_chunk, p2), F32)],
        scratch_shapes=[pltpu.VMEM((gb, rows, p2), F32)] * 3,
        compiler_params=_params(("parallel",)),
        name="ssm",
    )(u_g, x0_g, toep, bp, bps, cp, lam, dfull)


def _sb_logs(z, mask):
    sp = _softplus(z)
    lsig = z - sp
    if mask is not None:
        sp = jnp.where(mask, sp, 0.0)
    return lsig, sp, jnp.sum(sp, axis=1, keepdims=True)


def _sb_suffix(sp, tri):
    hi, lo = _split2(sp)
    return _dot(hi, tri) + _dot(lo, tri)


def _sb_weights(lsig, cs, carry, mask):
    att = jnp.exp(lsig - (cs + carry))
    if mask is not None:
        att = jnp.where(mask, att, 0.0)
    return att.astype(BF16)


def _sb_prompt_kernel(bias_ref, q_ref, k_ref, v_ref, tri_ref, o_ref, carry_scr, acc_scr, *, tile):
    hp = pl.program_id(1)
    qi = pl.program_id(2)
    q = q_ref[...] * (SB_HEAD_DIM ** -0.5)
    lane = lax.broadcasted_iota(jnp.int32, (1, LANES), 1)
    head_lanes = (lane < SB_HEAD_DIM, lane >= SB_HEAD_DIM)
    qh = [jnp.where(hl, q, 0.0).astype(BF16) for hl in head_lanes]
    bias = [bias_ref[hp * 2], bias_ref[hp * 2 + 1]]
    row = lax.broadcasted_iota(jnp.int32, (tile, tile), 0)
    col = lax.broadcasted_iota(jnp.int32, (tile, tile), 1)
    causal = col < row
    carry_scr[...] = jnp.zeros_like(carry_scr)
    acc_scr[...] = jnp.zeros_like(acc_scr)

    def blocks(kjs, mask):
        tri = tri_ref[...]
        carry = [carry_scr[0], carry_scr[1]]
        chains = [(i, hh) for i in range(len(kjs)) for hh in range(2)]
        k0s = [pl.multiple_of(kj * tile, tile) for kj in kjs]
        kbs = [k_ref[pl.ds(k0, tile), :].astype(BF16) for k0 in k0s]
        logs = [_sb_logs(_dot_nt(qh[hh], kbs[i]) + bias[hh], mask) for i, hh in chains]
        css = [_sb_suffix(lg[1], tri) for lg in logs]
        atts = []
        for (i, hh), lg, cs in zip(chains, logs, css):
            atts.append(_sb_weights(lg[0], cs, jnp.concatenate([carry[hh]] * (tile // LANES), axis=1), mask))
            carry[hh] = carry[hh] + lg[2]
        out = None
        for (i, hh), att in zip(chains, atts):
            v = v_ref[pl.ds(k0s[i], tile), :]
            o = _dot(att, jnp.where(head_lanes[hh], v, 0.0).astype(BF16))
            out = o if out is None else out + o
        carry_scr[0] = carry[0]
        carry_scr[1] = carry[1]
        acc_scr[...] += out

    blocks([qi], causal)

    def body(t, _):
        blocks([qi - 1 - 2 * t, qi - 2 - 2 * t], None)
        return 0

    lax.fori_loop(0, lax.shift_right_logical(qi, 1), body, 0)

    @pl.when((qi & 1) == 1)
    def _():
        blocks([0], None)

    o_ref[...] = acc_scr[...]


def _sb_prompt(bias, big, k2d, v2d, tri, bsz, seq):
    tile = min(SB_TILE, seq)
    nq = seq // tile
    m = bsz * seq
    kern = functools.partial(_sb_prompt_kernel, tile=tile)
    q_col0 = BLK // LANES
    return pl.pallas_call(
        kern,
        grid=(bsz, SB_HEADS // 2, nq),
        in_specs=[
            pl.BlockSpec(memory_space=pltpu.SMEM),
            pl.BlockSpec((tile, LANES), lambda b, h, i: (b * nq + i, q_col0 + h)),
            pl.BlockSpec((seq, LANES), lambda b, h, i: (b, h)),
            pl.BlockSpec((seq, LANES), lambda b, h, i: (b, h)),
            pl.BlockSpec((tile, tile), lambda b, h, i: (0, 0)),
        ],
        out_specs=pl.BlockSpec((tile, LANES), lambda b, h, i: (b * nq + i, h)),
        out_shape=jax.ShapeDtypeStruct((m, BLK), F32),
        scratch_shapes=[pltpu.VMEM((2, tile, LANES), F32), pltpu.VMEM((tile, LANES), F32)],
        compiler_params=_params(("parallel", "parallel", "arbitrary")),
        name="sb_prompt",
    )(bias, big, k2d, v2d, tri)


def _sb_paged_kernel(pt_ref, bias_ref, q_ref, kn_ref, vn_ref, tri_ref, *rest, n_q):
    del pt_ref
    npg = PAGES_PER_STEP
    k_refs = rest[:npg]
    v_refs = rest[npg:2 * npg]
    o_ref, qbd_scr, bias_scr, carry_scr, acc_scr = rest[2 * npg:]
    rows = SB_HEADS * n_q
    s = pl.program_id(1)

    @pl.when(s == 0)
    def _():
        row_head = _div_pow2(lax.broadcasted_iota(jnp.int32, (rows, BLK), 0), n_q)
        lane_head = _div_pow2(lax.broadcasted_iota(jnp.int32, (rows, BLK), 1), SB_HEAD_DIM)
        q_all = jnp.concatenate([q_ref[...]] * SB_HEADS, axis=0) * (SB_HEAD_DIM ** -0.5)
        qbd = jnp.where(row_head == lane_head, q_all, 0.0).astype(BF16)
        qbd_scr[...] = qbd
        rh = _div_pow2(lax.broadcasted_iota(jnp.int32, (rows, LANES), 0), n_q)
        bt = jnp.zeros((rows, LANES), F32)
        for h in range(SB_HEADS):
            bt = jnp.where(rh == h, bias_ref[h], bt)
        bias_scr[...] = bt
        qpos = lax.broadcasted_iota(jnp.int32, (rows, PAGE_SIZE), 0) & (n_q - 1)
        kpos = lax.broadcasted_iota(jnp.int32, (rows, PAGE_SIZE), 1)
        mask = kpos < qpos
        z = _dot(qbd, kn_ref[...].astype(BF16)) + bt
        lsig, sp, tot = _sb_logs(z, mask)
        att = _sb_weights(lsig, _sb_suffix(sp, tri_ref[...]), 0.0, mask)
        acc_scr[...] = _dot_nt(att, vn_ref[...].astype(BF16))
        carry_scr[...] = jnp.broadcast_to(tot, (rows, LANES))

    qbd = qbd_scr[...]
    bias = bias_scr[...]
    logs = [_sb_logs(_dot(qbd, k_refs[i][...].astype(BF16)) + bias, None) for i in range(npg)]
    cs_all = _sb_suffix(jnp.concatenate([l[1] for l in logs], axis=0), tri_ref[...])
    carry = carry_scr[...]
    out = None
    for i in range(npg):
        att = _sb_weights(logs[i][0], cs_all[i * rows:(i + 1) * rows], carry, None)
        o = _dot_nt(att, v_refs[i][...].astype(BF16))
        out = o if out is None else out + o
        carry = carry + logs[i][2]
    carry_scr[...] = carry
    acc_scr[...] += out

    @pl.when(s == pl.num_programs(1) - 1)
    def _():
        o_ref[...] = acc_scr[...]


def _sb_paged(page_table, bias, big, kn_t, vn_t, tri, cache_kt, cache_vt, layer, n_q):
    n_seq, n_pages = page_table.shape
    npg = PAGES_PER_STEP
    steps = n_pages // npg
    rows = SB_HEADS * n_q
    q_col = 1

    def page_spec(i):
        def imap(b, s, pt):
            return (layer, pt[b, n_pages - 1 - (s * npg + i)], 0, 0)
        return pl.BlockSpec((None, None, BLK, PAGE_SIZE), imap)

    grid_spec = pltpu.PrefetchScalarGridSpec(
        num_scalar_prefetch=1,
        grid=(n_seq, steps),
        in_specs=[
            pl.BlockSpec(memory_space=pltpu.SMEM),
            pl.BlockSpec((n_q, BLK), lambda b, s, pt: (b, q_col)),
            pl.BlockSpec((BLK, PAGE_SIZE), lambda b, s, pt: (b, 0)),
            pl.BlockSpec((BLK, PAGE_SIZE), lambda b, s, pt: (b, 0)),
            pl.BlockSpec((PAGE_SIZE, PAGE_SIZE), lambda b, s, pt: (0, 0)),
        ] + [page_spec(i) for i in range(npg)] + [page_spec(i) for i in range(npg)],
        out_specs=pl.BlockSpec((rows, BLK), lambda b, s, pt: (b, 0)),
        scratch_shapes=[pltpu.VMEM((rows, BLK), BF16), pltpu.VMEM((rows, LANES), F32),
                        pltpu.VMEM((rows, LANES), F32), pltpu.VMEM((rows, BLK), F32)],
    )
    kern = functools.partial(_sb_paged_kernel, n_q=n_q)
    return pl.pallas_call(
        kern,
        grid_spec=grid_spec,
        out_shape=jax.ShapeDtypeStruct((n_seq * rows, BLK), F32),
        compiler_params=_params(("parallel", "arbitrary")),
        name="sb_paged",
    )(page_table, bias, big, kn_t, vn_t, tri, *([cache_kt] * npg), *([cache_vt] * npg))


def _ret_kernel(q_ref, k_ref, v_ref, g_ref, cos_ref, sin_ref, dmask_ref, xi_ref, zeta_ref, gch_ref,
                s0_ref, gain_ref, o_ref, sout_ref, s_scr):
    c = pl.program_id(1)

    @pl.when(c == 0)
    def _():
        s_scr[...] = s0_ref[...]

    cos = cos_ref[...]
    sin = sin_ref[...]
    half = RET_HEAD_DIM // 2
    heads = range(RET_HEADS)
    sls = [slice(h * RET_HEAD_DIM, (h + 1) * RET_HEAD_DIM) for h in heads]
    rq_b, rk, v_b = [], [], []
    for h in heads:
        qh = q_ref[:, sls[h]]
        kh = k_ref[:, sls[h]]
        rq_b.append((qh * cos + pltpu.roll(qh, half, 1) * sin).astype(BF16))
        rk.append((kh * cos + pltpu.roll(kh, half, 1) * sin) * (RET_HEAD_DIM ** -0.5))
        v_b.append(v_ref[:, sls[h]].astype(BF16))
    states = [s_scr[h] for h in heads]
    inner = [(_dot_nt(rq_b[h], rk[h].astype(BF16)) * dmask_ref[h]).astype(BF16) for h in heads]
    cross = [_dot(rq_b[h], states[h].astype(BF16)) * xi_ref[h] for h in heads]
    outs = [_dot(inner[h], v_b[h]) + cross[h] for h in heads]
    for h in heads:
        kz = (rk[h] * zeta_ref[h]).astype(BF16)
        s_scr[h] = states[h] * gch_ref[h] + lax.dot_general(kz, v_b[h], _TN, preferred_element_type=F32)
    for h in heads:
        o = outs[h]
        mu = jnp.mean(o, axis=-1, keepdims=True)
        d = o - mu
        var = jnp.mean(d * d, axis=-1, keepdims=True)
        y = d * lax.rsqrt(var + EPS) * gain_ref[:, sls[h]]
        gate = g_ref[:, sls[h]]
        o_ref[:, sls[h]] = gate * jax.nn.sigmoid(gate) * y

    @pl.when(c == pl.num_programs(1) - 1)
    def _():
        sout_ref[...] = s_scr[...]


def _retention(big, cosf, sinf, dmask, xi, zeta, gch, s0, gain, bsz, seq):
    chunk = min(RET_CHUNK, seq)
    nc = seq // chunk
    m = bsz * seq
    hd = RET_HEAD_DIM
    sds = jax.ShapeDtypeStruct

    def col(j):
        return pl.BlockSpec((chunk, BLK), lambda b, c: (b * nc + c, j))

    def const3(shape):
        return pl.BlockSpec(shape, lambda b, c: (0, 0, 0))

    state_spec = pl.BlockSpec((None, RET_HEADS, hd, hd), lambda b, c: (b, 0, 0, 0))
    return pl.pallas_call(
        _ret_kernel,
        grid=(bsz, nc),
        in_specs=[
            col(4), col(5), col(6), col(7),
            pl.BlockSpec((chunk, hd), lambda b, c: (c, 0)),
            pl.BlockSpec((chunk, hd), lambda b, c: (c, 0)),
            const3((RET_HEADS, chunk, chunk)),
            const3((RET_HEADS, chunk, hd)),
            const3((RET_HEADS, chunk, hd)),
            const3((RET_HEADS, 1, hd)),
            state_spec,
            pl.BlockSpec((1, BLK), lambda b, c: (0, 0)),
        ],
        out_specs=[pl.BlockSpec((chunk, BLK), lambda b, c: (b * nc + c, 0)), state_spec],
        out_shape=[sds((m, BLK), F32), sds((bsz, RET_HEADS, hd, hd), F32)],
        scratch_shapes=[pltpu.VMEM((RET_HEADS, hd, hd), F32)],
        compiler_params=_params(("parallel", "arbitrary")),
        name="retention",
    )(big, big, big, big, cosf, sinf, dmask, xi, zeta, gch, s0, gain)


def _merge_kernel(x_ref, gmix_ref, ys_ref, ob_ref, oc_ref, wgate_ref, bgate_ref,
                  wglu_ref, wbs_ref, wbb_ref, wbr_ref, wo_ref, o_ref):
    x = x_ref[...]
    h = _rms_norm(x, gmix_ref[...]).astype(BF16)
    gates = jax.nn.sigmoid(_dot(h, wgate_ref[...]) + bgate_ref[...])
    glu = _dot(ys_ref[...].astype(BF16), wglu_ref[...])
    out_a = glu[:, :BLK] * jax.nn.sigmoid(glu[:, BLK:])
    d = D_MODEL
    merged = (gates[:, :d] * _dot(out_a.astype(BF16), wbs_ref[...])
              + gates[:, d:2 * d] * _dot(ob_ref[...].astype(BF16), wbb_ref[...])
              + gates[:, 2 * d:] * _dot(oc_ref[...].astype(BF16), wbr_ref[...]))
    o_ref[...] = x + _dot(merged.astype(BF16), wo_ref[...])


def _merge(x2d, gmix, ys, ob, oc, wgate, bgate, wglu, wbs, wbb, wbr, wo, tm):
    m = x2d.shape[0]

    def rows(width):
        return pl.BlockSpec((tm, width), lambda i: (i, 0))

    def whole(shape):
        return pl.BlockSpec(shape, lambda i: (0, 0), pipeline_mode=pl.Buffered(1))

    return pl.pallas_call(
        _merge_kernel,
        grid=(m // tm,),
        in_specs=[rows(D_MODEL), whole((1, D_MODEL)), rows(BLK), rows(BLK), rows(BLK),
                  whole((D_MODEL, GATE_WIDTH)), whole((1, GATE_WIDTH)),
                  whole((BLK, 2 * BLK)), whole((BLK, D_MODEL)), whole((BLK, D_MODEL)), whole((BLK, D_MODEL)),
                  whole((D_MODEL, D_MODEL))],
        out_specs=rows(D_MODEL),
        out_shape=jax.ShapeDtypeStruct((m, D_MODEL), F32),
        compiler_params=_params(("parallel",)),
        name="merge",
    )(x2d, gmix, ys, ob, oc, wgate, bgate, wglu, wbs, wbb, wbr, wo)


def _ffn_kernel(x_ref, g_ref, wup_ref, wdn_ref, o_ref, h_scr, acc_scr):
    f = pl.program_id(1)

    @pl.when(f == 0)
    def _():
        x = x_ref[...]
        h_scr[...] = _rms_norm(x, g_ref[...]).astype(BF16)
        acc_scr[...] = x

    up = jnp.maximum(_dot(h_scr[...], wup_ref[...]), 0.0)
    acc_scr[...] += _dot((up * up).astype(BF16), wdn_ref[...])

    @pl.when(f == pl.num_programs(1) - 1)
    def _():
        o_ref[...] = acc_scr[...]


def _ffn(x2d, g, wup, wdn, tm, tf):
    m = x2d.shape[0]
    return pl.pallas_call(
        _ffn_kernel,
        grid=(m // tm, D_FF // tf),
        in_specs=[
            pl.BlockSpec((tm, D_MODEL), lambda i, f: (i, 0)),
            pl.BlockSpec((1, D_MODEL), lambda i, f: (0, 0)),
            pl.BlockSpec((D_MODEL, tf), lambda i, f: (0, f)),
            pl.BlockSpec((tf, D_MODEL), lambda i, f: (f, 0)),
        ],
        out_specs=pl.BlockSpec((tm, D_MODEL), lambda i, f: (i, 0)),
        out_shape=jax.ShapeDtypeStruct((m, D_MODEL), F32),
        scratch_shapes=[pltpu.VMEM((tm, D_MODEL), BF16), pltpu.VMEM((tm, D_MODEL), F32)],
        compiler_params=_params(("parallel", "arbitrary")),
        name="ffn",
    )(x2d, g, wup, wdn)


def _tri(keys):
    return (jnp.arange(keys)[:, None] > jnp.arange(keys)[None, :]).astype(BF16)


def _rotary_tables(start, seq):
    half = RET_HEAD_DIM // 2
    pos = start + jnp.arange(seq, dtype=jnp.int32)
    inv_freq = ROPE_BASE ** (-jnp.arange(half, dtype=F32) / half)
    ang = pos.astype(F32)[:, None] * inv_freq[None, :]
    cos = jnp.cos(ang)
    sin = jnp.sin(ang)
    return jnp.concatenate([cos, cos], axis=1), jnp.concatenate([-sin, sin], axis=1)


def _decay_tables(chunk):
    log_g = jnp.log1p(-jnp.exp2(-5.0 - jnp.arange(RET_HEADS, dtype=F32)))
    i = jnp.arange(chunk, dtype=F32)
    diff = i[:, None] - i[None, :]
    dmask = jnp.where(diff[None] >= 0, jnp.exp(jnp.maximum(diff, 0.0)[None] * log_g[:, None, None]), 0.0)
    xi = jnp.exp((i[None, :] + 1.0) * log_g[:, None])
    zeta = jnp.exp((chunk - 1.0 - i)[None, :] * log_g[:, None])
    gch = jnp.exp(chunk * log_g)
    hd = RET_HEAD_DIM
    bc = lambda t: jnp.broadcast_to(t[:, :, None], (RET_HEADS, chunk, hd))
    return dmask, bc(xi), bc(zeta), jnp.broadcast_to(gch[:, None, None], (RET_HEADS, 1, hd))


def _ssm_tables(prep, t_len):
    assert t_len in (SSM_CHUNK, SSM_CHUNK // 2)
    toep, bpre, bpim, cpre, cpim, lam = prep
    tn = t_len * SSM_GROUP
    off = (SSM_CHUNK - t_len) * SSM_GROUP
    bp = jnp.concatenate([bpre, bpim], axis=-1)[:, off:, :]
    bps = jnp.concatenate([bpim, bpre], axis=-1)[:, off:, :]
    cp = jnp.concatenate([cpre, cpim], axis=-1)[:, :tn, :]
    r = 0 if t_len == SSM_CHUNK else 2
    lr, li = lam[:, r:r + 1, :], lam[:, r + 1:r + 2, :]
    rows = jnp.concatenate([jnp.concatenate([lr, lr], -1), jnp.concatenate([-li, li], -1)], axis=1)
    lam8 = jnp.concatenate([rows, jnp.zeros((rows.shape[0], 6, rows.shape[2]), F32)], axis=1)
    return toep[:, :tn, :tn], bp, bps, cp, lam8


def _layer(x3d, past, x0_re, x0_im, s0, lw, prep, consts, layer):
    bsz, seq, _ = x3d.shape
    m = bsz * seq
    x2d = x3d.reshape(m, D_MODEL)
    tm_big = min(1024, m)

    big, k2d, v2d = _in_proj(x2d, lw["gmix"], lw["w_in"], lw["qg"], lw["kg"], consts["bd"], tm_big)

    t_len = min(SSM_CHUNK, seq)
    nchunk = seq // t_len
    g, n = SSM_GROUPS, SSM_GROUP
    u_g = big[:, :BLK].reshape(bsz, nchunk, t_len, g, n).transpose(3, 1, 0, 2, 4).reshape(g, nchunk * bsz, t_len * n)
    x0 = jnp.stack([jnp.concatenate([x0_re, x0_im], axis=-1), jnp.concatenate([x0_im, x0_re], axis=-1)])
    x0_g = x0.transpose(2, 0, 1, 3)
    toep, bp, bps, cp, lam8 = _ssm_tables(prep, t_len)
    dfull = jnp.tile(lw["ssm_d"], (1, t_len))[:, None, :]
    y_g, xf_g = _ssm(u_g, x0_g, toep, bp, bps, cp, lam8, dfull, bsz)
    ys = y_g.reshape(g, nchunk, bsz, t_len, n).transpose(2, 1, 3, 0, 4).reshape(m, BLK)
    xf = xf_g.transpose(1, 0, 2)
    ssm_re, ssm_im = xf[..., :SSM_STATE], xf[..., SSM_STATE:]

    if past is None:
        ob = _sb_prompt(lw["sb_bias"], big, k2d, v2d, _tri(min(SB_TILE, seq)), bsz, seq)
        start = 0
    else:
        cache_kt, cache_vt, page_table = past

        def new_t(t):
            t = jnp.pad(t.reshape(bsz, seq, BLK), ((0, 0), (0, PAGE_SIZE - seq), (0, 0)))
            return t.transpose(0, 2, 1).reshape(bsz * BLK, PAGE_SIZE)

        o_hq = _sb_paged(page_table, lw["sb_bias"], big, new_t(k2d), new_t(v2d), consts["tri_page"],
                         cache_kt, cache_vt, layer, seq)
        o5 = o_hq.reshape(bsz, SB_HEADS, seq, SB_HEADS, SB_HEAD_DIM)
        ob = jnp.stack([o5[:, h, :, h, :] for h in range(SB_HEADS)], axis=2).reshape(m, BLK)
        start = page_table.shape[1] * PAGE_SIZE

    cosf, sinf = _rotary_tables(start, seq)
    dmask, xi, zeta, gch = _decay_tables(min(RET_CHUNK, seq))
    oc, ret_state = _retention(big, cosf, sinf, dmask, xi, zeta, gch, s0, lw["ret_gain"], bsz, seq)

    x1 = _merge(x2d, lw["gmix"], ys, ob, oc, lw["wgate"], lw["bgate"], lw["wglu"], lw["wbs"], lw["wbb"], lw["wbr"],
                lw["wo"], min(256, m))
    x2 = _ffn(x1, lw["gffn"], lw["wup"], lw["wdn"], tm_big, 1024)
    shape_kv = (bsz, seq, SB_HEADS, SB_HEAD_DIM)
    return x2.reshape(bsz, seq, D_MODEL), k2d.reshape(shape_kv), v2d.reshape(shape_kv), ssm_re, ssm_im, ret_state


def kernel(x_prompt, x_sample, cache_k, cache_v, state_ssm_re, state_ssm_im, state_ret, page_table, norm_mix, w_in, sb_q_norm, sb_k_norm, sb_logit_bias, ssm_lambda_re, ssm_lambda_im, ssm_log_dt, ssm_b_re, ssm_b_im, ssm_c_re, ssm_c_im, ssm_d, ssm_w_glu, ret_norm, w_branch_ssm, w_branch_sb, w_branch_ret, w_gate, b_gate, w_o, norm_ffn, w_ff_up, w_ff_down):
    depth = w_in.shape[0]
    n_prompt = x_prompt.shape[0]
    n_phys = cache_k.shape[1]
    cache_kt = cache_k.transpose(0, 1, 3, 4, 2).reshape(depth, n_phys, BLK, PAGE_SIZE)
    cache_vt = cache_v.transpose(0, 1, 3, 4, 2).reshape(depth, n_phys, BLK, PAGE_SIZE)
    lane_head = jnp.arange(BLK) // SB_HEAD_DIM
    consts = {
        "bd": jnp.where(lane_head[:, None] == lane_head[None, :], 1.0 / SB_HEAD_DIM, 0.0).astype(BF16),
        "tri_page": _tri(PAGE_SIZE),
    }
    zeros_ssm = jnp.zeros((n_prompt, SSM_GROUPS, SSM_STATE), F32)
    zeros_ret = jnp.zeros((n_prompt, RET_HEADS, RET_HEAD_DIM, RET_HEAD_DIM), F32)

    yp, ys = x_prompt, x_sample
    outs_p, outs_s = [], []
    for l in range(depth):
        lw = {
            "gmix": norm_mix[l][None, :],
            "w_in": w_in[l].astype(BF16),
            "qg": jnp.tile(sb_q_norm[l], SB_HEADS)[None, :],
            "kg": jnp.tile(sb_k_norm[l], SB_HEADS)[None, :],
            "sb_bias": sb_logit_bias[l],
            "ssm_d": ssm_d[l],
            "ret_gain": ret_norm[l][None, :],
            "wgate": w_gate[l].astype(BF16),
            "bgate": b_gate[l][None, :],
            "wglu": ssm_w_glu[l].astype(BF16),
            "wbs": w_branch_ssm[l].astype(BF16),
            "wbb": w_branch_sb[l].astype(BF16),
            "wbr": w_branch_ret[l].astype(BF16),
            "wo": w_o[l].astype(BF16),
            "gffn": norm_ffn[l][None, :],
            "wup": w_ff_up[l].astype(BF16),
            "wdn": w_ff_down[l].astype(BF16),
        }
        prep = _ssm_prep(ssm_lambda_re[l][:, None, :], ssm_lambda_im[l][:, None, :], ssm_log_dt[l][:, None, None],
                         ssm_b_re[l].transpose(0, 2, 1), ssm_b_im[l].transpose(0, 2, 1), ssm_c_re[l], ssm_c_im[l])
        rp = _layer(yp, None, zeros_ssm, zeros_ssm, zeros_ret, lw, prep, consts, l)
        rs = _layer(ys, (cache_kt, cache_vt, page_table), state_ssm_re[l], state_ssm_im[l], state_ret[l],
                    lw, prep, consts, l)
        yp, ys = rp[0], rs[0]
        outs_p.append(rp[1:])
        outs_s.append(rs[1:])
    stack = lambda outs, i: jnp.stack([o[i] for o in outs])
    return (yp, ys) + tuple(stack(outs_p, i) for i in range(5)) + tuple(stack(outs_s, i) for i in range(5))
```

```python
import functools

import jax
import jax.numpy as jnp
from jax import lax
from jax.experimental import pallas as pl
from jax.experimental.pallas import tpu as pltpu

F32 = jnp.float32
BF16 = jnp.bfloat16

EPS = 1e-6
D_MODEL = 1024
SSM_GROUP = 16
SSM_GROUPS = 32
SSM_STATE = 64
SSM_CHUNK = 16
SB_HEAD_DIM = 64
SB_HEADS = 8
RET_HEAD_DIM = 128
RET_HEADS = 4
RET_CHUNK = 128
ROPE_BASE = 10000.0
D_FF = 4 * D_MODEL
PAGE_SIZE = 128
BLK = 512
IN_WIDTH = 8 * BLK
GATE_WIDTH = 3 * D_MODEL
LANES = 128
SB_TILE = 256
PAGES_PER_STEP = 8
VMEM_LIMIT = 56 * 1024 * 1024

_NT = (((1,), (1,)), ((), ()))
_TN = (((0,), (0,)), ((), ()))


def _dot(a, b):
    return jnp.dot(a, b, preferred_element_type=F32)


def _dot_nt(a, b):
    return lax.dot_general(a, b, _NT, preferred_element_type=F32)


def _split2(x):
    hi = x.astype(BF16)
    lo = (x - hi.astype(F32)).astype(BF16)
    return hi, lo


def _dot_x3(a, b, nt=False):
    d = _dot_nt if nt else _dot
    ah, al = _split2(a)
    bh, bl = _split2(b)
    return d(ah, bh) + (d(ah, bl) + d(al, bh))


def _div_pow2(x, d):
    assert d & (d - 1) == 0
    return lax.shift_right_logical(x, jnp.int32(d.bit_length() - 1))


def _softplus(z):
    return jnp.maximum(z, 0.0) + jnp.log(1.0 + jnp.exp(-jnp.abs(z)))


def _rms_norm(x, g):
    ms = jnp.mean(x * x, axis=-1, keepdims=True)
    return x * lax.rsqrt(ms + EPS) * g


def _params(sem):
    return pltpu.CompilerParams(dimension_semantics=sem, vmem_limit_bytes=VMEM_LIMIT)


def _in_proj_kernel(x_ref, gmix_ref, w_ref, qg_ref, kg_ref, bd_ref, big_ref, u4_ref, k_ref, v_ref, h_scr):
    j = pl.program_id(1)

    @pl.when(j == 0)
    def _():
        h_scr[...] = _rms_norm(x_ref[...], gmix_ref[...]).astype(BF16)

    acc = _dot(h_scr[...], w_ref[...])

    @pl.when(j == 0)
    def _():
        for s in range(BLK // LANES):
            u4_ref[s] = acc[:, s * LANES:(s + 1) * LANES]

    def head_norm(a, g_ref):
        hi, lo = _split2(a * a)
        msq = _dot(hi, bd_ref[...]) + _dot(lo, bd_ref[...])
        return a * lax.rsqrt(msq + EPS) * g_ref[...]

    @pl.when(j == 1)
    def _():
        big_ref[...] = head_norm(acc, qg_ref)

    @pl.when(j == 2)
    def _():
        kn = head_norm(acc, kg_ref)
        big_ref[...] = kn
        k_ref[...] = kn

    @pl.when(j == 3)
    def _():
        big_ref[...] = acc
        v_ref[...] = acc

    @pl.when(j >= 4)
    def _():
        big_ref[...] = acc


def _in_proj(x2d, gmix, w_in, qg, kg, bd, tm):
    m = x2d.shape[0]
    sds = jax.ShapeDtypeStruct
    nslab = BLK // LANES
    return pl.pallas_call(
        _in_proj_kernel,
        grid=(m // tm, IN_WIDTH // BLK),
        in_specs=[
            pl.BlockSpec((tm, D_MODEL), lambda i, j: (i, 0)),
            pl.BlockSpec((1, D_MODEL), lambda i, j: (0, 0)),
            pl.BlockSpec((D_MODEL, BLK), lambda i, j: (0, j)),
            pl.BlockSpec((1, BLK), lambda i, j: (0, 0)),
            pl.BlockSpec((1, BLK), lambda i, j: (0, 0)),
            pl.BlockSpec((BLK, BLK), lambda i, j: (0, 0)),
        ],
        out_specs=[
            pl.BlockSpec((tm, BLK), lambda i, j: (i, jnp.maximum(j - 1, 0))),
            pl.BlockSpec((nslab, tm, LANES), lambda i, j: (0, i, 0)),
            pl.BlockSpec((tm, BLK), lambda i, j: (i, 0)),
            pl.BlockSpec((tm, BLK), lambda i, j: (i, 0)),
        ],
        out_shape=[sds((m, IN_WIDTH - BLK), F32), sds((nslab, m, LANES), F32), sds((m, BLK), F32), sds((m, BLK), F32)],
        scratch_shapes=[pltpu.VMEM((tm, D_MODEL), BF16)],
        compiler_params=_params(("parallel", "arbitrary")),
        name="in_proj",
    )(x2d, gmix, w_in, qg, kg, bd)


def _ssm_prep_kernel(lre_ref, lim_ref, ldt_ref, btre_ref, btim_ref, cre_ref, cim_ref,
                     toep_ref, bpre_ref, bpim_ref, cpre_ref, cpim_ref, lam_ref):
    t_len = SSM_CHUNK
    lre = lre_ref[...]
    lim = lim_ref[...]
    dt = jnp.exp(ldt_ref[...])
    a = lre * dt
    b = lim * dt
    ea = jnp.exp(a)
    lbr = ea * jnp.cos(b)
    lbi = ea * jnp.sin(b)
    den = lre * lre + lim * lim
    xr = lbr - 1.0
    cr = (xr * lre + lbi * lim) / den
    ci = (lbi * lre - xr * lim) / den
    btre = btre_ref[...]
    btim = btim_ref[...]
    bbr = cr * btre - ci * btim
    bbi = cr * btim + ci * btre
    cre = cre_ref[...]
    cim = cim_ref[...]

    one = jnp.ones_like(lre)
    zero = jnp.zeros_like(lre)
    pows = [(one, zero)]
    for _ in range(t_len):
        pr, pi = pows[-1]
        pows.append((pr * lbr - pi * lbi, pr * lbi + pi * lbr))

    def cmul(xre, xim, p):
        return xre * p[0] - xim * p[1], xre * p[1] + xim * p[0]

    def stack(parts):
        return jnp.concatenate(parts, axis=0)

    a_parts = [cmul(cre, cim, pows[t]) for t in range(t_len)]
    a1_parts = [cmul(cre, cim, pows[t + 1]) for t in range(t_len)]
    h_parts = [cmul(bbr, bbi, pows[t_len - 1 - s]) for s in range(t_len)]
    a_re = stack([p[0] for p in a_parts])
    a_im = stack([p[1] for p in a_parts])

    tn = t_len * SSM_GROUP
    r0 = _dot_x3(bbr, a_re, nt=True) - _dot_x3(bbi, a_im, nt=True)
    lane = lax.broadcasted_iota(jnp.int32, (SSM_GROUP, tn), 1)
    blocks = [r0]
    for s in range(1, t_len):
        blocks.append(jnp.where(lane >= s * SSM_GROUP, pltpu.roll(r0, s * SSM_GROUP, 1), 0.0))
    toep_ref[...] = stack(blocks)
    bpre_ref[...] = stack([p[0] for p in h_parts])
    bpim_ref[...] = stack([p[1] for p in h_parts])
    cpre_ref[...] = stack([p[0] for p in a1_parts])
    cpim_ref[...] = -stack([p[1] for p in a1_parts])
    half = t_len // 2
    lam_ref[...] = stack([pows[t_len][0], pows[t_len][1], pows[half][0], pows[half][1],
                          zero, zero, zero, zero])


def _ssm_prep(lre, lim, ldt, btre, btim, cre, cim):
    g = lre.shape[0]
    n, p = SSM_GROUP, SSM_STATE
    tn = SSM_CHUNK * n
    sds = jax.ShapeDtypeStruct

    def spec(r, c):
        return pl.BlockSpec((None, r, c), lambda i: (i, 0, 0))

    return pl.pallas_call(
        _ssm_prep_kernel,
        grid=(g,),
        in_specs=[spec(1, p), spec(1, p), spec(1, 1), spec(n, p), spec(n, p), spec(n, p), spec(n, p)],
        out_specs=[spec(tn, tn), spec(tn, p), spec(tn, p), spec(tn, p), spec(tn, p), spec(8, p)],
        out_shape=[sds((g, tn, tn), F32), sds((g, tn, p), F32), sds((g, tn, p), F32),
                   sds((g, tn, p), F32), sds((g, tn, p), F32), sds((g, 8, p), F32)],
        compiler_params=_params(("parallel",)),
        name="ssm_prep",
    )(lre, lim, ldt, btre, btim, cre, cim)


def _ssm_kernel(u_ref, x0_ref, toep_ref, bp_ref, bps_ref, cp_ref, lam_ref, d_ref, y_ref, xf_ref,
                w_scr, ws_scr, xall_scr, *, n_seq, n_chunks, t_len):
    assert n_seq == 1 or n_chunks == 1
    n = SSM_GROUP
    per = LANES // n
    halves = t_len // per
    lane_run = _div_pow2(lax.broadcasted_iota(jnp.int32, (1, LANES), 1), n)
    pieces = [u_ref[:, s * LANES:(s + 1) * LANES] for s in range(t_len)]

    def transpose_runs(vs):
        vs = list(vs)
        d = per // 2
        while d:
            low = (lane_run & d) == 0
            for i in range(per):
                if i & d == 0:
                    a, b = vs[i], vs[i + d]
                    vs[i] = jnp.where(low, a, pltpu.roll(b, d * n, 1))
                    vs[i + d] = jnp.where(low, pltpu.roll(a, LANES - d * n, 1), b)
            d //= 2
        return vs

    u_halves = [transpose_runs(pieces[k * per:(k + 1) * per]) for k in range(halves)]
    ugs = [u_halves[0][gl] if halves == 1 else jnp.concatenate([h[gl] for h in u_halves], axis=1)
           for gl in range(per)]

    for gl in range(per):
        w_scr[gl] = _dot_x3(ugs[gl], bp_ref[gl])
    for gl in range(per):
        ws_scr[gl] = _dot_x3(ugs[gl], bps_ref[gl])

    l1 = [jnp.broadcast_to(lam_ref[gl, 0:1, :], (n_seq, 2 * SSM_STATE)) for gl in range(per)]
    l2 = [jnp.broadcast_to(lam_ref[gl, 1:2, :], (n_seq, 2 * SSM_STATE)) for gl in range(per)]

    def step(rows, carry):
        out = []
        for gl in range(per):
            x, xs = carry[2 * gl], carry[2 * gl + 1]
            xall_scr[gl, rows, :] = x
            out.append(x * l1[gl] + xs * l2[gl] + w_scr[gl, rows, :])
            out.append(xs * l1[gl] - x * l2[gl] + ws_scr[gl, rows, :])
        return tuple(out)

    init = tuple(x0_ref[i] for i in range(2 * per))
    if n_chunks == 1:
        fin = step(slice(None), init)
    else:
        fin = lax.fori_loop(0, n_chunks, lambda c, carry: step(pl.ds(c, 1), carry), init)

    ys = []
    for gl in range(per):
        xf_ref[gl] = fin[2 * gl]
        y = (_dot_x3(ugs[gl], toep_ref[gl]) + _dot_x3(xall_scr[gl], cp_ref[gl], nt=True)
             + ugs[gl] * d_ref[gl])
        y = jax.nn.gelu(y)
        ys.append([y[:, k * LANES:(k + 1) * LANES] for k in range(halves)])
    for k in range(halves):
        back = transpose_runs([ys[gl][k] for gl in range(per)])
        for j in range(per):
            s = k * per + j
            y_ref[:, s * LANES:(s + 1) * LANES] = back[j]


def _ssm(u4, x0, toep, bp, bps, cp, lam, dfull, n_seq, n_chunks, t_len):
    nslab, nblk, rows, width = u4.shape
    per = LANES // SSM_GROUP
    p2 = 2 * SSM_STATE
    tn = t_len * SSM_GROUP
    sds = jax.ShapeDtypeStruct

    def table(*shape):
        return pl.BlockSpec((per,) + shape, lambda s, b: (s,) + (0,) * len(shape))

    kern = functools.partial(_ssm_kernel, n_seq=n_seq, n_chunks=n_chunks, t_len=t_len)
    return pl.pallas_call(
        kern,
        grid=(nslab, nblk),
        in_specs=[pl.BlockSpec((None, None, rows, width), lambda s, b: (s, b, 0, 0)),
                  pl.BlockSpec((None, 2 * per, n_seq, p2), lambda s, b: (b, s, 0, 0)),
                  table(tn, tn), table(tn, p2), table(tn, p2), table(tn, p2), table(8, p2), table(1, tn)],
        out_specs=[pl.BlockSpec((None, None, rows, width), lambda s, b: (s, b, 0, 0)),
                   pl.BlockSpec((None, per, n_seq, p2), lambda s, b: (b, s, 0, 0))],
        out_shape=[sds(u4.shape, F32), sds((nblk, nslab * per, n_seq, p2), F32)],
        scratch_shapes=[pltpu.VMEM((per, rows, p2), F32)] * 3,
        compiler_params=_params(("parallel", "parallel")),
        name="ssm",
    )(u4, x0, toep, bp, bps, cp, lam, dfull)


def _sb_logs(z, mask):
    sp = _softplus(z)
    lsig = z - sp
    if mask is not None:
        sp = jnp.where(mask, sp, 0.0)
    return lsig, sp, jnp.sum(sp, axis=1, keepdims=True)


def _sb_suffix(sp, tri):
    hi, lo = _split2(sp)
    return _dot(hi, tri) + _dot(lo, tri)


def _sb_weights(lsig, cs, carry, mask):
    att = jnp.exp(lsig - (cs + carry))
    if mask is not None:
        att = jnp.where(mask, att, 0.0)
    return att.astype(BF16)


def _sb_prompt_kernel(bias_ref, q_ref, k_ref, v_ref, tri_ref, o_ref, carry_scr, acc_scr, *, tile):
    hp = pl.program_id(1)
    qi = pl.program_id(2)
    q = q_ref[...] * (SB_HEAD_DIM ** -0.5)
    lane = lax.broadcasted_iota(jnp.int32, (1, LANES), 1)
    head_lanes = (lane < SB_HEAD_DIM, lane >= SB_HEAD_DIM)
    qh = [jnp.where(hl, q, 0.0).astype(BF16) for hl in head_lanes]
    bias = [bias_ref[hp * 2], bias_ref[hp * 2 + 1]]
    row = lax.broadcasted_iota(jnp.int32, (tile, tile), 0)
    col = lax.broadcasted_iota(jnp.int32, (tile, tile), 1)
    causal = col < row
    carry_scr[...] = jnp.zeros_like(carry_scr)
    acc_scr[...] = jnp.zeros_like(acc_scr)

    def blocks(kjs, mask):
        tri = tri_ref[...]
        carry = [carry_scr[0], carry_scr[1]]
        chains = [(i, hh) for i in range(len(kjs)) for hh in range(2)]
        k0s = [pl.multiple_of(kj * tile, tile) for kj in kjs]
        kbs = [k_ref[pl.ds(k0, tile), :].astype(BF16) for k0 in k0s]
        logs = [_sb_logs(_dot_nt(qh[hh], kbs[i]) + bias[hh], mask) for i, hh in chains]
        css = [_sb_suffix(lg[1], tri) for lg in logs]
        atts = []
        for (i, hh), lg, cs in zip(chains, logs, css):
            atts.append(_sb_weights(lg[0], cs, jnp.concatenate([carry[hh]] * (tile // LANES), axis=1), mask))
            carry[hh] = carry[hh] + lg[2]
        out = None
        for (i, hh), att in zip(chains, atts):
            v = v_ref[pl.ds(k0s[i], tile), :]
            o = _dot(att, jnp.where(head_lanes[hh], v, 0.0).astype(BF16))
            out = o if out is None else out + o
        carry_scr[0] = carry[0]
        carry_scr[1] = carry[1]
        acc_scr[...] += out

    blocks([qi], causal)

    def body(t, _):
        blocks([qi - 1 - 2 * t, qi - 2 - 2 * t], None)
        return 0

    lax.fori_loop(0, lax.shift_right_logical(qi, 1), body, 0)

    @pl.when((qi & 1) == 1)
    def _():
        blocks([0], None)

    o_ref[...] = acc_scr[...]


def _sb_prompt(bias, big, k2d, v2d, tri, bsz, seq):
    tile = min(SB_TILE, seq)
    nq = seq // tile
    m = bsz * seq
    kern = functools.partial(_sb_prompt_kernel, tile=tile)
    q_col0 = 0
    return pl.pallas_call(
        kern,
        grid=(bsz, SB_HEADS // 2, nq),
        in_specs=[
            pl.BlockSpec(memory_space=pltpu.SMEM),
            pl.BlockSpec((tile, LANES), lambda b, h, i: (b * nq + i, q_col0 + h)),
            pl.BlockSpec((seq, LANES), lambda b, h, i: (b, h)),
            pl.BlockSpec((seq, LANES), lambda b, h, i: (b, h)),
            pl.BlockSpec((tile, tile), lambda b, h, i: (0, 0)),
        ],
        out_specs=pl.BlockSpec((tile, LANES), lambda b, h, i: (b * nq + i, h)),
        out_shape=jax.ShapeDtypeStruct((m, BLK), F32),
        scratch_shapes=[pltpu.VMEM((2, tile, LANES), F32), pltpu.VMEM((tile, LANES), F32)],
        compiler_params=_params(("parallel", "parallel", "arbitrary")),
        name="sb_prompt",
    )(bias, big, k2d, v2d, tri)


def _sb_paged_kernel(pt_ref, bias_ref, q_ref, kn_ref, vn_ref, tri_ref, *rest, n_q):
    del pt_ref
    npg = PAGES_PER_STEP
    k_refs = rest[:npg]
    v_refs = rest[npg:2 * npg]
    o_ref, qbd_scr, bias_scr, carry_scr, acc_scr = rest[2 * npg:]
    rows = SB_HEADS * n_q
    s = pl.program_id(1)

    @pl.when(s == 0)
    def _():
        row_head = _div_pow2(lax.broadcasted_iota(jnp.int32, (rows, BLK), 0), n_q)
        lane_head = _div_pow2(lax.broadcasted_iota(jnp.int32, (rows, BLK), 1), SB_HEAD_DIM)
        q_all = jnp.concatenate([q_ref[...]] * SB_HEADS, axis=0) * (SB_HEAD_DIM ** -0.5)
        qbd = jnp.where(row_head == lane_head, q_all, 0.0).astype(BF16)
        qbd_scr[...] = qbd
        rh = _div_pow2(lax.broadcasted_iota(jnp.int32, (rows, LANES), 0), n_q)
        bt = jnp.zeros((rows, LANES), F32)
        for h in range(SB_HEADS):
            bt = jnp.where(rh == h, bias_ref[h], bt)
        bias_scr[...] = bt
        qpos = lax.broadcasted_iota(jnp.int32, (rows, PAGE_SIZE), 0) & (n_q - 1)
        kpos = lax.broadcasted_iota(jnp.int32, (rows, PAGE_SIZE), 1)
        mask = kpos < qpos
        z = _dot(qbd, kn_ref[...].astype(BF16)) + bt
        lsig, sp, tot = _sb_logs(z, mask)
        att = _sb_weights(lsig, _sb_suffix(sp, tri_ref[...]), 0.0, mask)
        acc_scr[...] = _dot_nt(att, vn_ref[...].astype(BF16))
        carry_scr[...] = jnp.broadcast_to(tot, (rows, LANES))

    kcat = jnp.concatenate([k_refs[i][...].astype(BF16) for i in range(npg)], axis=1)
    z = _dot(qbd_scr[...], kcat) + jnp.concatenate([bias_scr[...]] * npg, axis=1)
    sp = _softplus(z)
    lsig = z - sp
    sp_rows = jnp.concatenate([sp[:, i * PAGE_SIZE:(i + 1) * PAGE_SIZE] for i in range(npg)], axis=0)
    cs_rows = _sb_suffix(sp_rows, tri_ref[...])
    tot_rows = jnp.sum(sp_rows, axis=1, keepdims=True)
    carry = carry_scr[...]
    after = []
    for i in range(npg):
        after.append(cs_rows[i * rows:(i + 1) * rows] + carry)
        carry = carry + tot_rows[i * rows:(i + 1) * rows]
    carry_scr[...] = carry
    att = jnp.exp(lsig - jnp.concatenate(after, axis=1)).astype(BF16)
    vcat = jnp.concatenate([v_refs[i][...].astype(BF16) for i in range(npg)], axis=1)
    acc_scr[...] += _dot_nt(att, vcat)

    @pl.when(s == pl.num_programs(1) - 1)
    def _():
        o_ref[...] = acc_scr[...]


def _sb_paged(page_table, bias, big, kn_t, vn_t, tri, cache_kt, cache_vt, layer, n_q):
    n_seq, n_pages = page_table.shape
    npg = PAGES_PER_STEP
    steps = n_pages // npg
    rows = SB_HEADS * n_q
    q_col = 0

    def page_spec(i):
        def imap(b, s, pt):
            return (layer, pt[b, n_pages - 1 - (s * npg + i)], 0, 0)
        return pl.BlockSpec((None, None, BLK, PAGE_SIZE), imap)

    grid_spec = pltpu.PrefetchScalarGridSpec(
        num_scalar_prefetch=1,
        grid=(n_seq, steps),
        in_specs=[
            pl.BlockSpec(memory_space=pltpu.SMEM),
            pl.BlockSpec((n_q, BLK), lambda b, s, pt: (b, q_col)),
            pl.BlockSpec((BLK, PAGE_SIZE), lambda b, s, pt: (b, 0)),
            pl.BlockSpec((BLK, PAGE_SIZE), lambda b, s, pt: (b, 0)),
            pl.BlockSpec((PAGE_SIZE, PAGE_SIZE), lambda b, s, pt: (0, 0)),
        ] + [page_spec(i) for i in range(npg)] + [page_spec(i) for i in range(npg)],
        out_specs=pl.BlockSpec((rows, BLK), lambda b, s, pt: (b, 0)),
        scratch_shapes=[pltpu.VMEM((rows, BLK), BF16), pltpu.VMEM((rows, LANES), F32),
                        pltpu.VMEM((rows, LANES), F32), pltpu.VMEM((rows, BLK), F32)],
    )
    kern = functools.partial(_sb_paged_kernel, n_q=n_q)
    return pl.pallas_call(
        kern,
        grid_spec=grid_spec,
        out_shape=jax.ShapeDtypeStruct((n_seq * rows, BLK), F32),
        compiler_params=_params(("parallel", "arbitrary")),
        name="sb_paged",
    )(page_table, bias, big, kn_t, vn_t, tri, *([cache_kt] * npg), *([cache_vt] * npg))


def _ret_kernel(q_ref, k_ref, v_ref, g_ref, cos_ref, sin_ref, dmask_ref, xi_ref, zeta_ref, gch_ref,
                s0_ref, gain_ref, o_ref, sout_ref, s_scr):
    c = pl.program_id(1)

    @pl.when(c == 0)
    def _():
        s_scr[...] = s0_ref[...]

    cos = cos_ref[...]
    sin = sin_ref[...]
    half = RET_HEAD_DIM // 2
    heads = range(RET_HEADS)
    sls = [slice(h * RET_HEAD_DIM, (h + 1) * RET_HEAD_DIM) for h in heads]
    rq_b, rk, v_b = [], [], []
    for h in heads:
        qh = q_ref[:, sls[h]]
        kh = k_ref[:, sls[h]]
        rq_b.append((qh * cos + pltpu.roll(qh, half, 1) * sin).astype(BF16))
        rk.append((kh * cos + pltpu.roll(kh, half, 1) * sin) * (RET_HEAD_DIM ** -0.5))
        v_b.append(v_ref[:, sls[h]].astype(BF16))
    states = [s_scr[h] for h in heads]
    inner = [(_dot_nt(rq_b[h], rk[h].astype(BF16)) * dmask_ref[h]).astype(BF16) for h in heads]
    cross = [_dot(rq_b[h], states[h].astype(BF16)) * xi_ref[h] for h in heads]
    outs = [_dot(inner[h], v_b[h]) + cross[h] for h in heads]
    for h in heads:
        kz = (rk[h] * zeta_ref[h]).astype(BF16)
        s_scr[h] = states[h] * gch_ref[h] + lax.dot_general(kz, v_b[h], _TN, preferred_element_type=F32)
    for h in heads:
        o = outs[h]
        mu = jnp.mean(o, axis=-1, keepdims=True)
        d = o - mu
        var = jnp.mean(d * d, axis=-1, keepdims=True)
        y = d * lax.rsqrt(var + EPS) * gain_ref[:, sls[h]]
        gate = g_ref[:, sls[h]]
        o_ref[:, sls[h]] = gate * jax.nn.sigmoid(gate) * y

    @pl.when(c == pl.num_programs(1) - 1)
    def _():
        sout_ref[...] = s_scr[...]


def _retention(big, cosf, sinf, dmask, xi, zeta, gch, s0, gain, bsz, seq):
    chunk = min(RET_CHUNK, seq)
    nc = seq // chunk
    m = bsz * seq
    hd = RET_HEAD_DIM
    sds = jax.ShapeDtypeStruct

    def col(j):
        return pl.BlockSpec((chunk, BLK), lambda b, c: (b * nc + c, j))

    def const3(shape):
        return pl.BlockSpec(shape, lambda b, c: (0, 0, 0))

    state_spec = pl.BlockSpec((None, RET_HEADS, hd, hd), lambda b, c: (b, 0, 0, 0))
    return pl.pallas_call(
        _ret_kernel,
        grid=(bsz, nc),
        in_specs=[
            col(3), col(4), col(5), col(6),
            pl.BlockSpec((chunk, hd), lambda b, c: (c, 0)),
            pl.BlockSpec((chunk, hd), lambda b, c: (c, 0)),
            const3((RET_HEADS, chunk, chunk)),
            const3((RET_HEADS, chunk, hd)),
            const3((RET_HEADS, chunk, hd)),
            const3((RET_HEADS, 1, hd)),
            state_spec,
            pl.BlockSpec((1, BLK), lambda b, c: (0, 0)),
        ],
        out_specs=[pl.BlockSpec((chunk, BLK), lambda b, c: (b * nc + c, 0)), state_spec],
        out_shape=[sds((m, BLK), F32), sds((bsz, RET_HEADS, hd, hd), F32)],
        scratch_shapes=[pltpu.VMEM((RET_HEADS, hd, hd), F32)],
        compiler_params=_params(("parallel", "arbitrary")),
        name="retention",
    )(big, big, big, big, cosf, sinf, dmask, xi, zeta, gch, s0, gain)


def _merge_kernel(x_ref, gmix_ref, ys_ref, ob_ref, oc_ref, wgate_ref, bgate_ref,
                  wglu_ref, wbs_ref, wbb_ref, wbr_ref, wo_ref, o_ref):
    x = x_ref[...]
    h = _rms_norm(x, gmix_ref[...]).astype(BF16)
    gates = jax.nn.sigmoid(_dot(h, wgate_ref[...]) + bgate_ref[...])
    ys = jnp.concatenate([ys_ref[s] for s in range(BLK // LANES)], axis=1)
    glu = _dot(ys.astype(BF16), wglu_ref[...])
    out_a = glu[:, :BLK] * jax.nn.sigmoid(glu[:, BLK:])
    d = D_MODEL
    merged = (gates[:, :d] * _dot(out_a.astype(BF16), wbs_ref[...])
              + gates[:, d:2 * d] * _dot(ob_ref[...].astype(BF16), wbb_ref[...])
              + gates[:, 2 * d:] * _dot(oc_ref[...].astype(BF16), wbr_ref[...]))
    o_ref[...] = x + _dot(merged.astype(BF16), wo_ref[...])


def _merge(x2d, gmix, ys, ob, oc, wgate, bgate, wglu, wbs, wbb, wbr, wo, tm):
    m = x2d.shape[0]

    def rows(width):
        return pl.BlockSpec((tm, width), lambda i: (i, 0))

    def whole(shape):
        return pl.BlockSpec(shape, lambda i: (0, 0), pipeline_mode=pl.Buffered(1))

    return pl.pallas_call(
        _merge_kernel,
        grid=(m // tm,),
        in_specs=[rows(D_MODEL), whole((1, D_MODEL)),
                  pl.BlockSpec((BLK // LANES, tm, LANES), lambda i: (0, i, 0)), rows(BLK), rows(BLK),
                  whole((D_MODEL, GATE_WIDTH)), whole((1, GATE_WIDTH)),
                  whole((BLK, 2 * BLK)), whole((BLK, D_MODEL)), whole((BLK, D_MODEL)), whole((BLK, D_MODEL)),
                  whole((D_MODEL, D_MODEL))],
        out_specs=rows(D_MODEL),
        out_shape=jax.ShapeDtypeStruct((m, D_MODEL), F32),
        compiler_params=_params(("parallel",)),
        name="merge",
    )(x2d, gmix, ys, ob, oc, wgate, bgate, wglu, wbs, wbb, wbr, wo)


def _ffn_kernel(x_ref, g_ref, wup_ref, wdn_ref, o_ref, h_scr, acc_scr):
    f = pl.program_id(1)

    @pl.when(f == 0)
    def _():
        x = x_ref[...]
        h_scr[...] = _rms_norm(x, g_ref[...]).astype(BF16)
        acc_scr[...] = x

    up = jnp.maximum(_dot(h_scr[...], wup_ref[...]), 0.0)
    acc_scr[...] += _dot((up * up).astype(BF16), wdn_ref[...])

    @pl.when(f == pl.num_programs(1) - 1)
    def _():
        o_ref[...] = acc_scr[...]


def _ffn(x2d, g, wup, wdn, tm, tf):
    m = x2d.shape[0]
    return pl.pallas_call(
        _ffn_kernel,
        grid=(m // tm, D_FF // tf),
        in_specs=[
            pl.BlockSpec((tm, D_MODEL), lambda i, f: (i, 0)),
            pl.BlockSpec((1, D_MODEL), lambda i, f: (0, 0)),
            pl.BlockSpec((D_MODEL, tf), lambda i, f: (0, f)),
            pl.BlockSpec((tf, D_MODEL), lambda i, f: (f, 0)),
        ],
        out_specs=pl.BlockSpec((tm, D_MODEL), lambda i, f: (i, 0)),
        out_shape=jax.ShapeDtypeStruct((m, D_MODEL), F32),
        scratch_shapes=[pltpu.VMEM((tm, D_MODEL), BF16), pltpu.VMEM((tm, D_MODEL), F32)],
        compiler_params=_params(("parallel", "arbitrary")),
        name="ffn",
    )(x2d, g, wup, wdn)


def _tri(keys):
    return (jnp.arange(keys)[:, None] > jnp.arange(keys)[None, :]).astype(BF16)


def _rotary_tables(start, seq):
    half = RET_HEAD_DIM // 2
    pos = start + jnp.arange(seq, dtype=jnp.int32)
    inv_freq = ROPE_BASE ** (-jnp.arange(half, dtype=F32) / half)
    ang = pos.astype(F32)[:, None] * inv_freq[None, :]
    cos = jnp.cos(ang)
    sin = jnp.sin(ang)
    return jnp.concatenate([cos, cos], axis=1), jnp.concatenate([-sin, sin], axis=1)


def _decay_tables(chunk):
    log_g = jnp.log1p(-jnp.exp2(-5.0 - jnp.arange(RET_HEADS, dtype=F32)))
    i = jnp.arange(chunk, dtype=F32)
    diff = i[:, None] - i[None, :]
    dmask = jnp.where(diff[None] >= 0, jnp.exp(jnp.maximum(diff, 0.0)[None] * log_g[:, None, None]), 0.0)
    xi = jnp.exp((i[None, :] + 1.0) * log_g[:, None])
    zeta = jnp.exp((chunk - 1.0 - i)[None, :] * log_g[:, None])
    gch = jnp.exp(chunk * log_g)
    hd = RET_HEAD_DIM
    bc = lambda t: jnp.broadcast_to(t[:, :, None], (RET_HEADS, chunk, hd))
    return dmask, bc(xi), bc(zeta), jnp.broadcast_to(gch[:, None, None], (RET_HEADS, 1, hd))


def _ssm_tables(prep, t_len):
    assert t_len in (SSM_CHUNK, SSM_CHUNK // 2)
    toep, bpre, bpim, cpre, cpim, lam = prep
    tn = t_len * SSM_GROUP
    off = (SSM_CHUNK - t_len) * SSM_GROUP
    bp = jnp.concatenate([bpre, bpim], axis=-1)[:, off:, :]
    bps = jnp.concatenate([bpim, bpre], axis=-1)[:, off:, :]
    cp = jnp.concatenate([cpre, cpim], axis=-1)[:, :tn, :]
    r = 0 if t_len == SSM_CHUNK else 2
    lr, li = lam[:, r:r + 1, :], lam[:, r + 1:r + 2, :]
    rows = jnp.concatenate([jnp.concatenate([lr, lr], -1), jnp.concatenate([-li, li], -1)], axis=1)
    lam8 = jnp.concatenate([rows, jnp.zeros((rows.shape[0], 6, rows.shape[2]), F32)], axis=1)
    return toep[:, :tn, :tn], bp, bps, cp, lam8


def _layer(x3d, past, x0_re, x0_im, s0, lw, prep, consts, layer):
    bsz, seq, _ = x3d.shape
    m = bsz * seq
    x2d = x3d.reshape(m, D_MODEL)
    tm_big = min(1024, m)

    big, u4, k2d, v2d = _in_proj(x2d, lw["gmix"], lw["w_in"], lw["qg"], lw["kg"], consts["bd"], tm_big)

    t_len = min(SSM_CHUNK, seq)
    nchunk = seq // t_len
    n_seq = bsz if nchunk == 1 else 1
    nblk = bsz // n_seq
    g = SSM_GROUPS
    nslab = BLK // LANES
    x0 = jnp.stack([jnp.concatenate([x0_re, x0_im], axis=-1), jnp.concatenate([x0_im, x0_re], axis=-1)], axis=2)
    x0 = x0.reshape(nblk, n_seq, 2 * g, 2 * SSM_STATE).transpose(0, 2, 1, 3)
    toep, bp, bps, cp, lam8 = _ssm_tables(prep, t_len)
    dfull = jnp.tile(lw["ssm_d"], (1, t_len))[:, None, :]
    y4, xf = _ssm(u4.reshape(nslab, nblk, n_seq * nchunk, t_len * LANES), x0, toep, bp, bps, cp, lam8, dfull,
                  n_seq, nchunk, t_len)
    y4 = y4.reshape(nslab, m, LANES)
    xf = xf.transpose(0, 2, 1, 3).reshape(bsz, g, 2 * SSM_STATE)
    ssm_re, ssm_im = xf[..., :SSM_STATE], xf[..., SSM_STATE:]

    if past is None:
        ob = _sb_prompt(lw["sb_bias"], big, k2d, v2d, _tri(min(SB_TILE, seq)), bsz, seq)
        start = 0
    else:
        cache_kt, cache_vt, page_table = past

        def new_t(t):
            t = jnp.pad(t.reshape(bsz, seq, BLK), ((0, 0), (0, PAGE_SIZE - seq), (0, 0)))
            return t.transpose(0, 2, 1).reshape(bsz * BLK, PAGE_SIZE)

        o_hq = _sb_paged(page_table, lw["sb_bias"], big, new_t(k2d), new_t(v2d), consts["tri_page"],
                         cache_kt, cache_vt, layer, seq)
        o5 = o_hq.reshape(bsz, SB_HEADS, seq, SB_HEADS, SB_HEAD_DIM)
        ob = jnp.stack([o5[:, h, :, h, :] for h in range(SB_HEADS)], axis=2).reshape(m, BLK)
        start = page_table.shape[1] * PAGE_SIZE

    cosf, sinf = _rotary_tables(start, seq)
    dmask, xi, zeta, gch = _decay_tables(min(RET_CHUNK, seq))
    oc, ret_state = _retention(big, cosf, sinf, dmask, xi, zeta, gch, s0, lw["ret_gain"], bsz, seq)

    x1 = _merge(x2d, lw["gmix"], y4, ob, oc, lw["wgate"], lw["bgate"], lw["wglu"], lw["wbs"], lw["wbb"], lw["wbr"],
                lw["wo"], min(256, m))
    x2 = _ffn(x1, lw["gffn"], lw["wup"], lw["wdn"], tm_big, 1024)
    shape_kv = (bsz, seq, SB_HEADS, SB_HEAD_DIM)
    return x2.reshape(bsz, seq, D_MODEL), k2d.reshape(shape_kv), v2d.reshape(shape_kv), ssm_re, ssm_im, ret_state


def kernel(x_prompt, x_sample, cache_k, cache_v, state_ssm_re, state_ssm_im, state_ret, page_table, norm_mix, w_in, sb_q_norm, sb_k_norm, sb_logit_bias, ssm_lambda_re, ssm_lambda_im, ssm_log_dt, ssm_b_re, ssm_b_im, ssm_c_re, ssm_c_im, ssm_d, ssm_w_glu, ret_norm, w_branch_ssm, w_branch_sb, w_branch_ret, w_gate, b_gate, w_o, norm_ffn, w_ff_up, w_ff_down):
    depth = w_in.shape[0]
    n_prompt = x_prompt.shape[0]
    n_phys = cache_k.shape[1]
    cache_kt = cache_k.transpose(0, 1, 3, 4, 2).reshape(depth, n_phys, BLK, PAGE_SIZE)
    cache_vt = cache_v.transpose(0, 1, 3, 4, 2).reshape(depth, n_phys, BLK, PAGE_SIZE)
    lane_head = jnp.arange(BLK) // SB_HEAD_DIM
    consts = {
        "bd": jnp.where(lane_head[:, None] == lane_head[None, :], 1.0 / SB_HEAD_DIM, 0.0).astype(BF16),
        "tri_page": _tri(PAGE_SIZE),
    }
    zeros_ssm = jnp.zeros((n_prompt, SSM_GROUPS, SSM_STATE), F32)
    zeros_ret = jnp.zeros((n_prompt, RET_HEADS, RET_HEAD_DIM, RET_HEAD_DIM), F32)

    yp, ys = x_prompt, x_sample
    outs_p, outs_s = [], []
    for l in range(depth):
        lw = {
            "gmix": norm_mix[l][None, :],
            "w_in": w_in[l].astype(BF16),
            "qg": jnp.tile(sb_q_norm[l], SB_HEADS)[None, :],
            "kg": jnp.tile(sb_k_norm[l], SB_HEADS)[None, :],
            "sb_bias": sb_logit_bias[l],
            "ssm_d": ssm_d[l],
            "ret_gain": ret_norm[l][None, :],
            "wgate": w_gate[l].astype(BF16),
            "bgate": b_gate[l][None, :],
            "wglu": ssm_w_glu[l].astype(BF16),
            "wbs": w_branch_ssm[l].astype(BF16),
            "wbb": w_branch_sb[l].astype(BF16),
            "wbr": w_branch_ret[l].astype(BF16),
            "wo": w_o[l].astype(BF16),
            "gffn": norm_ffn[l][None, :],
            "wup": w_ff_up[l].astype(BF16),
            "wdn": w_ff_down[l].astype(BF16),
        }
        prep = _ssm_prep(ssm_lambda_re[l][:, None, :], ssm_lambda_im[l][:, None, :], ssm_log_dt[l][:, None, None],
                         ssm_b_re[l].transpose(0, 2, 1), ssm_b_im[l].transpose(0, 2, 1), ssm_c_re[l], ssm_c_im[l])
        rp = _layer(yp, None, zeros_ssm, zeros_ssm, zeros_ret, lw, prep, consts, l)
        rs = _layer(ys, (cache_kt, cache_vt, page_table), state_ssm_re[l], state_ssm_im[l], state_ret[l],
                    lw, prep, consts, l)
        yp, ys = rp[0], rs[0]
        outs_p.append(rp[1:])
        outs_s.append(rs[1:])
    stack = lambda outs, i: jnp.stack([o[i] for o in outs])
    return (yp, ys) + tuple(stack(outs_p, i) for i in range(5)) + tuple(stack(outs_s, i) for i in range(5))
```

```python
import functools

import jax
import jax.numpy as jnp
from jax import lax
from jax.experimental import pallas as pl
from jax.experimental.pallas import tpu as pltpu

F32 = jnp.float32
BF16 = jnp.bfloat16

EPS = 1e-6
D_MODEL = 1024
SSM_GROUP = 16
SSM_GROUPS = 32
SSM_STATE = 64
SSM_CHUNK = 16
SB_HEAD_DIM = 64
SB_HEADS = 8
RET_HEAD_DIM = 128
RET_HEADS = 4
RET_CHUNK = 128
ROPE_BASE = 10000.0
D_FF = 4 * D_MODEL
PAGE_SIZE = 128
BLK = 512
IN_WIDTH = 8 * BLK
GATE_WIDTH = 3 * D_MODEL
LANES = 128
SB_TILE = 256
PAGES_PER_STEP = 8
PAGE_SLOTS = 3
VMEM_LIMIT = 56 * 1024 * 1024

_NT = (((1,), (1,)), ((), ()))
_TN = (((0,), (0,)), ((), ()))


def _dot(a, b):
    return jnp.dot(a, b, preferred_element_type=F32)


def _dot_nt(a, b):
    return lax.dot_general(a, b, _NT, preferred_element_type=F32)


def _split2(x):
    hi = x.astype(BF16)
    lo = (x - hi.astype(F32)).astype(BF16)
    return hi, lo


def _dot_x3(a, b, nt=False):
    d = _dot_nt if nt else _dot
    ah, al = _split2(a)
    bh, bl = _split2(b)
    return d(ah, bh) + (d(ah, bl) + d(al, bh))


def _div_pow2(x, d):
    assert d & (d - 1) == 0
    return lax.shift_right_logical(x, jnp.int32(d.bit_length() - 1))


def _softplus(z):
    return jnp.maximum(z, 0.0) + jnp.log(1.0 + jnp.exp(-jnp.abs(z)))


def _rms_norm(x, g):
    ms = jnp.mean(x * x, axis=-1, keepdims=True)
    return x * lax.rsqrt(ms + EPS) * g


def _params(sem):
    return pltpu.CompilerParams(dimension_semantics=sem, vmem_limit_bytes=VMEM_LIMIT)


def _in_proj_kernel(x_ref, gmix_ref, w_ref, qg_ref, kg_ref, bd_ref, big_ref, u4_ref, k_ref, v_ref, h_scr):
    j = pl.program_id(1)

    @pl.when(j == 0)
    def _():
        h_scr[...] = _rms_norm(x_ref[...], gmix_ref[...]).astype(BF16)

    acc = _dot(h_scr[...], w_ref[...])

    @pl.when(j == 0)
    def _():
        for s in range(BLK // LANES):
            u4_ref[s] = acc[:, s * LANES:(s + 1) * LANES]

    def head_norm(a, g_ref):
        hi, lo = _split2(a * a)
        msq = _dot(hi, bd_ref[...]) + _dot(lo, bd_ref[...])
        return a * lax.rsqrt(msq + EPS) * g_ref[...]

    @pl.when(j == 1)
    def _():
        big_ref[...] = head_norm(acc, qg_ref)

    @pl.when(j == 2)
    def _():
        kn = head_norm(acc, kg_ref)
        big_ref[...] = kn
        k_ref[...] = kn

    @pl.when(j == 3)
    def _():
        big_ref[...] = acc
        v_ref[...] = acc

    @pl.when(j >= 4)
    def _():
        big_ref[...] = acc


def _in_proj(x2d, gmix, w_in, qg, kg, bd, tm):
    m = x2d.shape[0]
    sds = jax.ShapeDtypeStruct
    nslab = BLK // LANES
    return pl.pallas_call(
        _in_proj_kernel,
        grid=(m // tm, IN_WIDTH // BLK),
        in_specs=[
            pl.BlockSpec((tm, D_MODEL), lambda i, j: (i, 0)),
            pl.BlockSpec((1, D_MODEL), lambda i, j: (0, 0)),
            pl.BlockSpec((D_MODEL, BLK), lambda i, j: (0, j)),
            pl.BlockSpec((1, BLK), lambda i, j: (0, 0)),
            pl.BlockSpec((1, BLK), lambda i, j: (0, 0)),
            pl.BlockSpec((BLK, BLK), lambda i, j: (0, 0)),
        ],
        out_specs=[
            pl.BlockSpec((tm, BLK), lambda i, j: (i, jnp.maximum(j - 1, 0))),
            pl.BlockSpec((nslab, tm, LANES), lambda i, j: (0, i, 0)),
            pl.BlockSpec((tm, BLK), lambda i, j: (i, 0)),
            pl.BlockSpec((tm, BLK), lambda i, j: (i, 0)),
        ],
        out_shape=[sds((m, IN_WIDTH - BLK), F32), sds((nslab, m, LANES), F32), sds((m, BLK), F32), sds((m, BLK), F32)],
        scratch_shapes=[pltpu.VMEM((tm, D_MODEL), BF16)],
        compiler_params=_params(("parallel", "arbitrary")),
        name="in_proj",
    )(x2d, gmix, w_in, qg, kg, bd)


def _ssm_prep_kernel(lre_ref, lim_ref, ldt_ref, btre_ref, btim_ref, cre_ref, cim_ref,
                     toep_ref, bpre_ref, bpim_ref, cpre_ref, cpim_ref, lam_ref):
    t_len = SSM_CHUNK
    lre = lre_ref[...]
    lim = lim_ref[...]
    dt = jnp.exp(ldt_ref[...])
    a = lre * dt
    b = lim * dt
    ea = jnp.exp(a)
    lbr = ea * jnp.cos(b)
    lbi = ea * jnp.sin(b)
    den = lre * lre + lim * lim
    xr = lbr - 1.0
    cr = (xr * lre + lbi * lim) / den
    ci = (lbi * lre - xr * lim) / den
    btre = btre_ref[...]
    btim = btim_ref[...]
    bbr = cr * btre - ci * btim
    bbi = cr * btim + ci * btre
    cre = cre_ref[...]
    cim = cim_ref[...]

    one = jnp.ones_like(lre)
    zero = jnp.zeros_like(lre)
    pows = [(one, zero)]
    for _ in range(t_len):
        pr, pi = pows[-1]
        pows.append((pr * lbr - pi * lbi, pr * lbi + pi * lbr))

    def cmul(xre, xim, p):
        return xre * p[0] - xim * p[1], xre * p[1] + xim * p[0]

    def stack(parts):
        return jnp.concatenate(parts, axis=0)

    a_parts = [cmul(cre, cim, pows[t]) for t in range(t_len)]
    a1_parts = [cmul(cre, cim, pows[t + 1]) for t in range(t_len)]
    h_parts = [cmul(bbr, bbi, pows[t_len - 1 - s]) for s in range(t_len)]
    a_re = stack([p[0] for p in a_parts])
    a_im = stack([p[1] for p in a_parts])

    tn = t_len * SSM_GROUP
    r0 = _dot_x3(bbr, a_re, nt=True) - _dot_x3(bbi, a_im, nt=True)
    lane = lax.broadcasted_iota(jnp.int32, (SSM_GROUP, tn), 1)
    blocks = [r0]
    for s in range(1, t_len):
        blocks.append(jnp.where(lane >= s * SSM_GROUP, pltpu.roll(r0, s * SSM_GROUP, 1), 0.0))
    toep_ref[...] = stack(blocks)
    bpre_ref[...] = stack([p[0] for p in h_parts])
    bpim_ref[...] = stack([p[1] for p in h_parts])
    cpre_ref[...] = stack([p[0] for p in a1_parts])
    cpim_ref[...] = -stack([p[1] for p in a1_parts])
    half = t_len // 2
    lam_ref[...] = stack([pows[t_len][0], pows[t_len][1], pows[half][0], pows[half][1],
                          zero, zero, zero, zero])


def _ssm_prep(lre, lim, ldt, btre, btim, cre, cim):
    g = lre.shape[0]
    n, p = SSM_GROUP, SSM_STATE
    tn = SSM_CHUNK * n
    sds = jax.ShapeDtypeStruct

    def spec(r, c):
        return pl.BlockSpec((None, r, c), lambda i: (i, 0, 0))

    return pl.pallas_call(
        _ssm_prep_kernel,
        grid=(g,),
        in_specs=[spec(1, p), spec(1, p), spec(1, 1), spec(n, p), spec(n, p), spec(n, p), spec(n, p)],
        out_specs=[spec(tn, tn), spec(tn, p), spec(tn, p), spec(tn, p), spec(tn, p), spec(8, p)],
        out_shape=[sds((g, tn, tn), F32), sds((g, tn, p), F32), sds((g, tn, p), F32),
                   sds((g, tn, p), F32), sds((g, tn, p), F32), sds((g, 8, p), F32)],
        compiler_params=_params(("parallel",)),
        name="ssm_prep",
    )(lre, lim, ldt, btre, btim, cre, cim)


def _ssm_kernel(u_ref, x0_ref, toep_ref, bp_ref, bps_ref, cp_ref, lam_ref, d_ref, y_ref, xf_ref,
                w_scr, ws_scr, xall_scr, *, n_seq, n_chunks, t_len):
    assert n_seq == 1 or n_chunks == 1
    n = SSM_GROUP
    per = LANES // n
    halves = t_len // per
    lane_run = _div_pow2(lax.broadcasted_iota(jnp.int32, (1, LANES), 1), n)
    rows = n_seq * n_chunks
    pieces = [u_ref[pl.ds(s, rows, stride=t_len), :] for s in range(t_len)]

    def transpose_runs(vs):
        vs = list(vs)
        d = per // 2
        while d:
            low = (lane_run & d) == 0
            for i in range(per):
                if i & d == 0:
                    a, b = vs[i], vs[i + d]
                    vs[i] = jnp.where(low, a, pltpu.roll(b, d * n, 1))
                    vs[i + d] = jnp.where(low, pltpu.roll(a, LANES - d * n, 1), b)
            d //= 2
        return vs

    u_halves = [transpose_runs(pieces[k * per:(k + 1) * per]) for k in range(halves)]
    ugs = [u_halves[0][gl] if halves == 1 else jnp.concatenate([h[gl] for h in u_halves], axis=1)
           for gl in range(per)]

    for gl in range(per):
        w_scr[gl] = _dot_x3(ugs[gl], bp_ref[gl])
    for gl in range(per):
        ws_scr[gl] = _dot_x3(ugs[gl], bps_ref[gl])

    l1 = [jnp.broadcast_to(lam_ref[gl, 0:1, :], (n_seq, 2 * SSM_STATE)) for gl in range(per)]
    l2 = [jnp.broadcast_to(lam_ref[gl, 1:2, :], (n_seq, 2 * SSM_STATE)) for gl in range(per)]

    def step(rows, carry):
        out = []
        for gl in range(per):
            x, xs = carry[2 * gl], carry[2 * gl + 1]
            xall_scr[gl, rows, :] = x
            out.append(x * l1[gl] + xs * l2[gl] + w_scr[gl, rows, :])
            out.append(xs * l1[gl] - x * l2[gl] + ws_scr[gl, rows, :])
        return tuple(out)

    init = tuple(x0_ref[i] for i in range(2 * per))
    if n_chunks == 1:
        fin = step(slice(None), init)
    else:
        fin = lax.fori_loop(0, n_chunks, lambda c, carry: step(pl.ds(c, 1), carry), init)

    ys = []
    for gl in range(per):
        xf_ref[gl] = fin[2 * gl]
        y = (_dot_x3(ugs[gl], toep_ref[gl]) + _dot_x3(xall_scr[gl], cp_ref[gl], nt=True)
             + ugs[gl] * d_ref[gl])
        y = jax.nn.gelu(y)
        ys.append([y[:, k * LANES:(k + 1) * LANES] for k in range(halves)])
    for k in range(halves):
        back = transpose_runs([ys[gl][k] for gl in range(per)])
        for j in range(per):
            y_ref[pl.ds(k * per + j, rows, stride=t_len), :] = back[j]


def _ssm(u4, x0, toep, bp, bps, cp, lam, dfull, n_seq, n_chunks, t_len):
    nslab, nblk, tokens, width = u4.shape
    rows = n_seq * n_chunks
    per = LANES // SSM_GROUP
    p2 = 2 * SSM_STATE
    tn = t_len * SSM_GROUP
    sds = jax.ShapeDtypeStruct

    def table(*shape):
        return pl.BlockSpec((per,) + shape, lambda s, b: (s,) + (0,) * len(shape))

    kern = functools.partial(_ssm_kernel, n_seq=n_seq, n_chunks=n_chunks, t_len=t_len)
    return pl.pallas_call(
        kern,
        grid=(nslab, nblk),
        in_specs=[pl.BlockSpec((None, None, tokens, width), lambda s, b: (s, b, 0, 0)),
                  pl.BlockSpec((None, 2 * per, n_seq, p2), lambda s, b: (b, s, 0, 0)),
                  table(tn, tn), table(tn, p2), table(tn, p2), table(tn, p2), table(8, p2), table(1, tn)],
        out_specs=[pl.BlockSpec((None, None, tokens, width), lambda s, b: (s, b, 0, 0)),
                   pl.BlockSpec((None, per, n_seq, p2), lambda s, b: (b, s, 0, 0))],
        out_shape=[sds(u4.shape, F32), sds((nblk, nslab * per, n_seq, p2), F32)],
        scratch_shapes=[pltpu.VMEM((per, rows, p2), F32)] * 3,
        compiler_params=_params(("parallel", "parallel")),
        name="ssm",
    )(u4, x0, toep, bp, bps, cp, lam, dfull)


def _sb_logs(z, mask):
    sp = _softplus(z)
    lsig = z - sp
    if mask is not None:
        sp = jnp.where(mask, sp, 0.0)
    return lsig, sp, jnp.sum(sp, axis=1, keepdims=True)


def _sb_suffix(sp, tri):
    hi, lo = _split2(sp)
    return _dot(hi, tri) + _dot(lo, tri)


def _sb_weights(lsig, cs, carry, mask):
    att = jnp.exp(lsig - (cs + carry))
    if mask is not None:
        att = jnp.where(mask, att, 0.0)
    return att.astype(BF16)


def _sb_prompt_kernel(bias_ref, q_ref, k_ref, v_ref, tri_ref, o_ref, carry_scr, acc_scr, *, tile):
    hp = pl.program_id(1)
    qi = pl.program_id(2)
    q = q_ref[...] * (SB_HEAD_DIM ** -0.5)
    lane = lax.broadcasted_iota(jnp.int32, (1, LANES), 1)
    head_lanes = (lane < SB_HEAD_DIM, lane >= SB_HEAD_DIM)
    qh = [jnp.where(hl, q, 0.0).astype(BF16) for hl in head_lanes]
    bias = [bias_ref[hp * 2], bias_ref[hp * 2 + 1]]
    row = lax.broadcasted_iota(jnp.int32, (tile, tile), 0)
    col = lax.broadcasted_iota(jnp.int32, (tile, tile), 1)
    causal = col < row
    carry_scr[...] = jnp.zeros_like(carry_scr)
    acc_scr[...] = jnp.zeros_like(acc_scr)

    def blocks(kjs, mask):
        tri = tri_ref[...]
        carry = [carry_scr[0], carry_scr[1]]
        chains = [(i, hh) for i in range(len(kjs)) for hh in range(2)]
        k0s = [pl.multiple_of(kj * tile, tile) for kj in kjs]
        kbs = [k_ref[pl.ds(k0, tile), :].astype(BF16) for k0 in k0s]
        logs = [_sb_logs(_dot_nt(qh[hh], kbs[i]) + bias[hh], mask) for i, hh in chains]
        css = [_sb_suffix(lg[1], tri) for lg in logs]
        atts = []
        for (i, hh), lg, cs in zip(chains, logs, css):
            atts.append(_sb_weights(lg[0], cs, jnp.concatenate([carry[hh]] * (tile // LANES), axis=1), mask))
            carry[hh] = carry[hh] + lg[2]
        out = None
        for (i, hh), att in zip(chains, atts):
            v = v_ref[pl.ds(k0s[i], tile), :]
            o = _dot(att, jnp.where(head_lanes[hh], v, 0.0).astype(BF16))
            out = o if out is None else out + o
        carry_scr[0] = carry[0]
        carry_scr[1] = carry[1]
        acc_scr[...] += out

    blocks([qi], causal)

    def body(t, _):
        blocks([qi - 1 - 2 * t, qi - 2 - 2 * t], None)
        return 0

    lax.fori_loop(0, lax.shift_right_logical(qi, 1), body, 0)

    @pl.when((qi & 1) == 1)
    def _():
        blocks([0], None)

    o_ref[...] = acc_scr[...]


def _sb_prompt(bias, big, k2d, v2d, tri, bsz, seq):
    tile = min(SB_TILE, seq)
    nq = seq // tile
    m = bsz * seq
    kern = functools.partial(_sb_prompt_kernel, tile=tile)
    q_col0 = 0
    return pl.pallas_call(
        kern,
        grid=(bsz, SB_HEADS // 2, nq),
        in_specs=[
            pl.BlockSpec(memory_space=pltpu.SMEM),
            pl.BlockSpec((tile, LANES), lambda b, h, i: (b * nq + i, q_col0 + h)),
            pl.BlockSpec((seq, LANES), lambda b, h, i: (b, h)),
            pl.BlockSpec((seq, LANES), lambda b, h, i: (b, h)),
            pl.BlockSpec((tile, tile), lambda b, h, i: (0, 0)),
        ],
        out_specs=pl.BlockSpec((tile, LANES), lambda b, h, i: (b * nq + i, h)),
        out_shape=jax.ShapeDtypeStruct((m, BLK), F32),
        scratch_shapes=[pltpu.VMEM((2, tile, LANES), F32), pltpu.VMEM((tile, LANES), F32)],
        compiler_params=_params(("parallel", "parallel", "arbitrary")),
        name="sb_prompt",
    )(bias, big, k2d, v2d, tri)


def _sb_paged_kernel(pt_ref, bias_ref, q_ref, kn_ref, vn_ref, tri_ref, *rest, n_q, layer, n_seq, steps, n_pages):
    npg = PAGES_PER_STEP
    ck_hbm, cv_hbm, o_ref, kbuf, vbuf, sems, qbd_scr, bias_scr, carry_scr, acc_scr = rest
    rows = SB_HEADS * n_q
    s = pl.program_id(1)
    t = pl.program_id(0) * steps + s
    total = n_seq * steps

    def page_copies(tt):
        slot = lax.rem(tt, PAGE_SLOTS)
        bb = lax.div(tt, steps)
        ss = tt - bb * steps
        copies = []
        for i in range(npg):
            page = pt_ref[bb, n_pages - 1 - (ss * npg + i)]
            copies.append(pltpu.make_async_copy(ck_hbm.at[layer, page], kbuf.at[slot, i], sems.at[slot, 0]))
            copies.append(pltpu.make_async_copy(cv_hbm.at[layer, page], vbuf.at[slot, i], sems.at[slot, 1]))
        return copies

    @pl.when(t == 0)
    def _():
        for d in range(min(PAGE_SLOTS - 1, total)):
            for c in page_copies(jnp.int32(d)):
                c.start()

    @pl.when(t + (PAGE_SLOTS - 1) < total)
    def _():
        for c in page_copies(t + (PAGE_SLOTS - 1)):
            c.start()

    @pl.when(s == 0)
    def _():
        row_head = _div_pow2(lax.broadcasted_iota(jnp.int32, (rows, BLK), 0), n_q)
        lane_head = _div_pow2(lax.broadcasted_iota(jnp.int32, (rows, BLK), 1), SB_HEAD_DIM)
        q_all = jnp.concatenate([q_ref[...]] * SB_HEADS, axis=0) * (SB_HEAD_DIM ** -0.5)
        qbd = jnp.where(row_head == lane_head, q_all, 0.0).astype(BF16)
        qbd_scr[...] = qbd
        rh = _div_pow2(lax.broadcasted_iota(jnp.int32, (rows, LANES), 0), n_q)
        bt = jnp.zeros((rows, LANES), F32)
        for h in range(SB_HEADS):
            bt = jnp.where(rh == h, bias_ref[h], bt)
        bias_scr[...] = bt
        qpos = lax.broadcasted_iota(jnp.int32, (rows, PAGE_SIZE), 0) & (n_q - 1)
        kpos = lax.broadcasted_iota(jnp.int32, (rows, PAGE_SIZE), 1)
        mask = kpos < qpos
        pad = jnp.zeros((PAGE_SIZE - n_q, BLK), F32)
        kn = jnp.concatenate([kn_ref[...], pad], axis=0).astype(BF16)
        vn = jnp.concatenate([vn_ref[...], pad], axis=0).astype(BF16)
        z = _dot_nt(qbd, kn) + bt
        lsig, sp, tot = _sb_logs(z, mask)
        att = _sb_weights(lsig, _sb_suffix(sp, tri_ref[...]), 0.0, mask)
        acc_scr[...] = _dot(att, vn)
        carry_scr[...] = jnp.broadcast_to(tot, (rows, LANES))

    for c in page_copies(t):
        c.wait()
    slot = lax.rem(t, PAGE_SLOTS)
    k_pages = [kbuf[slot, i] for i in range(npg)]
    v_pages = [vbuf[slot, i] for i in range(npg)]

    kcat = jnp.concatenate([kp.astype(BF16) for kp in k_pages], axis=1)
    z = _dot(qbd_scr[...], kcat) + jnp.concatenate([bias_scr[...]] * npg, axis=1)
    sp = _softplus(z)
    lsig = z - sp
    sp_rows = jnp.concatenate([sp[:, i * PAGE_SIZE:(i + 1) * PAGE_SIZE] for i in range(npg)], axis=0)
    cs_rows = _sb_suffix(sp_rows, tri_ref[...])
    tot_rows = jnp.sum(sp_rows, axis=1, keepdims=True)
    carry = carry_scr[...]
    after = []
    for i in range(npg):
        after.append(cs_rows[i * rows:(i + 1) * rows] + carry)
        carry = carry + tot_rows[i * rows:(i + 1) * rows]
    carry_scr[...] = carry
    att = jnp.exp(lsig - jnp.concatenate(after, axis=1)).astype(BF16)
    vcat = jnp.concatenate([vp.astype(BF16) for vp in v_pages], axis=1)
    acc_scr[...] += _dot_nt(att, vcat)

    @pl.when(s == pl.num_programs(1) - 1)
    def _():
        lane_head = _div_pow2(lax.broadcasted_iota(jnp.int32, (n_q, BLK), 1), SB_HEAD_DIM)
        out = acc_scr[0:n_q, :]
        for h in range(1, SB_HEADS):
            out = jnp.where(lane_head == h, acc_scr[h * n_q:(h + 1) * n_q, :], out)
        o_ref[...] = out


def _sb_paged(page_table, bias, big, k_new, v_new, tri, cache_kt, cache_vt, layer, n_q):
    n_seq, n_pages = page_table.shape
    npg = PAGES_PER_STEP
    steps = n_pages // npg
    rows = SB_HEADS * n_q
    q_col = 0
    ring = (PAGE_SLOTS, npg, BLK, PAGE_SIZE)

    grid_spec = pltpu.PrefetchScalarGridSpec(
        num_scalar_prefetch=1,
        grid=(n_seq, steps),
        in_specs=[
            pl.BlockSpec(memory_space=pltpu.SMEM),
            pl.BlockSpec((n_q, BLK), lambda b, s, pt: (b, q_col)),
            pl.BlockSpec((n_q, BLK), lambda b, s, pt: (b, 0)),
            pl.BlockSpec((n_q, BLK), lambda b, s, pt: (b, 0)),
            pl.BlockSpec((PAGE_SIZE, PAGE_SIZE), lambda b, s, pt: (0, 0)),
            pl.BlockSpec(memory_space=pl.ANY),
            pl.BlockSpec(memory_space=pl.ANY),
        ],
        out_specs=pl.BlockSpec((n_q, BLK), lambda b, s, pt: (b, 0)),
        scratch_shapes=[pltpu.VMEM(ring, F32), pltpu.VMEM(ring, F32), pltpu.SemaphoreType.DMA((PAGE_SLOTS, 2)),
                        pltpu.VMEM((rows, BLK), BF16), pltpu.VMEM((rows, LANES), F32),
                        pltpu.VMEM((rows, LANES), F32), pltpu.VMEM((rows, BLK), F32)],
    )
    kern = functools.partial(_sb_paged_kernel, n_q=n_q, layer=layer, n_seq=n_seq, steps=steps, n_pages=n_pages)
    return pl.pallas_call(
        kern,
        grid_spec=grid_spec,
        out_shape=jax.ShapeDtypeStruct((n_seq * n_q, BLK), F32),
        compiler_params=_params(("arbitrary", "arbitrary")),
        name="sb_paged",
    )(page_table, bias, big, k_new, v_new, tri, cache_kt, cache_vt)


def _ret_kernel(q_ref, k_ref, v_ref, g_ref, cos_ref, sin_ref, dmask_ref, xi_ref, zeta_ref, gch_ref,
                s0_ref, gain_ref, o_ref, sout_ref, s_scr):
    c = pl.program_id(1)

    @pl.when(c == 0)
    def _():
        s_scr[...] = s0_ref[...]

    cos = cos_ref[...]
    sin = sin_ref[...]
    half = RET_HEAD_DIM // 2
    heads = range(RET_HEADS)
    sls = [slice(h * RET_HEAD_DIM, (h + 1) * RET_HEAD_DIM) for h in heads]
    rq_b, rk, v_b = [], [], []
    for h in heads:
        qh = q_ref[:, sls[h]]
        kh = k_ref[:, sls[h]]
        rq_b.append((qh * cos + pltpu.roll(qh, half, 1) * sin).astype(BF16))
        rk.append((kh * cos + pltpu.roll(kh, half, 1) * sin) * (RET_HEAD_DIM ** -0.5))
        v_b.append(v_ref[:, sls[h]].astype(BF16))
    states = [s_scr[h] for h in heads]
    inner = [(_dot_nt(rq_b[h], rk[h].astype(BF16)) * dmask_ref[h]).astype(BF16) for h in heads]
    cross = [_dot(rq_b[h], states[h].astype(BF16)) * xi_ref[h] for h in heads]
    outs = [_dot(inner[h], v_b[h]) + cross[h] for h in heads]
    for h in heads:
        kz = (rk[h] * zeta_ref[h]).astype(BF16)
        s_scr[h] = states[h] * gch_ref[h] + lax.dot_general(kz, v_b[h], _TN, preferred_element_type=F32)
    for h in heads:
        o = outs[h]
        mu = jnp.mean(o, axis=-1, keepdims=True)
        d = o - mu
        var = jnp.mean(d * d, axis=-1, keepdims=True)
        y = d * lax.rsqrt(var + EPS) * gain_ref[:, sls[h]]
        gate = g_ref[:, sls[h]]
        o_ref[:, sls[h]] = gate * jax.nn.sigmoid(gate) * y

    @pl.when(c == pl.num_programs(1) - 1)
    def _():
        sout_ref[...] = s_scr[...]


def _retention(big, cosf, sinf, dmask, xi, zeta, gch, s0, gain, bsz, seq):
    chunk = min(RET_CHUNK, seq)
    nc = seq // chunk
    m = bsz * seq
    hd = RET_HEAD_DIM
    sds = jax.ShapeDtypeStruct

    def col(j):
        return pl.BlockSpec((chunk, BLK), lambda b, c: (b * nc + c, j))

    def const3(shape):
        return pl.BlockSpec(shape, lambda b, c: (0, 0, 0))

    state_spec = pl.BlockSpec((None, RET_HEADS, hd, hd), lambda b, c: (b, 0, 0, 0))
    return pl.pallas_call(
        _ret_kernel,
        grid=(bsz, nc),
        in_specs=[
            col(3), col(4), col(5), col(6),
            pl.BlockSpec((chunk, hd), lambda b, c: (c, 0)),
            pl.BlockSpec((chunk, hd), lambda b, c: (c, 0)),
            const3((RET_HEADS, chunk, chunk)),
            const3((RET_HEADS, chunk, hd)),
            const3((RET_HEADS, chunk, hd)),
            const3((RET_HEADS, 1, hd)),
            state_spec,
            pl.BlockSpec((1, BLK), lambda b, c: (0, 0)),
        ],
        out_specs=[pl.BlockSpec((chunk, BLK), lambda b, c: (b * nc + c, 0)), state_spec],
        out_shape=[sds((m, BLK), F32), sds((bsz, RET_HEADS, hd, hd), F32)],
        scratch_shapes=[pltpu.VMEM((RET_HEADS, hd, hd), F32)],
        compiler_params=_params(("parallel", "arbitrary")),
        name="retention",
    )(big, big, big, big, cosf, sinf, dmask, xi, zeta, gch, s0, gain)


def _merge_kernel(x_ref, gmix_ref, ys_ref, ob_ref, oc_ref, wgate_ref, bgate_ref,
                  wglu_ref, wbs_ref, wbb_ref, wbr_ref, wo_ref, o_ref):
    x = x_ref[...]
    h = _rms_norm(x, gmix_ref[...]).astype(BF16)
    gates = jax.nn.sigmoid(_dot(h, wgate_ref[...]) + bgate_ref[...])
    ys = jnp.concatenate([ys_ref[s] for s in range(BLK // LANES)], axis=1)
    glu = _dot(ys.astype(BF16), wglu_ref[...])
    out_a = glu[:, :BLK] * jax.nn.sigmoid(glu[:, BLK:])
    d = D_MODEL
    merged = (gates[:, :d] * _dot(out_a.astype(BF16), wbs_ref[...])
              + gates[:, d:2 * d] * _dot(ob_ref[...].astype(BF16), wbb_ref[...])
              + gates[:, 2 * d:] * _dot(oc_ref[...].astype(BF16), wbr_ref[...]))
    o_ref[...] = x + _dot(merged.astype(BF16), wo_ref[...])


def _merge(x2d, gmix, ys, ob, oc, wgate, bgate, wglu, wbs, wbb, wbr, wo, tm):
    m = x2d.shape[0]

    def rows(width):
        return pl.BlockSpec((tm, width), lambda i: (i, 0))

    def whole(shape):
        return pl.BlockSpec(shape, lambda i: (0, 0), pipeline_mode=pl.Buffered(1))

    return pl.pallas_call(
        _merge_kernel,
        grid=(m // tm,),
        in_specs=[rows(D_MODEL), whole((1, D_MODEL)),
                  pl.BlockSpec((BLK // LANES, tm, LANES), lambda i: (0, i, 0)), rows(BLK), rows(BLK),
                  whole((D_MODEL, GATE_WIDTH)), whole((1, GATE_WIDTH)),
                  whole((BLK, 2 * BLK)), whole((BLK, D_MODEL)), whole((BLK, D_MODEL)), whole((BLK, D_MODEL)),
                  whole((D_MODEL, D_MODEL))],
        out_specs=rows(D_MODEL),
        out_shape=jax.ShapeDtypeStruct((m, D_MODEL), F32),
        compiler_params=_params(("parallel",)),
        name="merge",
    )(x2d, gmix, ys, ob, oc, wgate, bgate, wglu, wbs, wbb, wbr, wo)


def _ffn_kernel(x_ref, g_ref, wup_ref, wdn_ref, o_ref, h_scr, acc_scr):
    f = pl.program_id(1)

    @pl.when(f == 0)
    def _():
        x = x_ref[...]
        h_scr[...] = _rms_norm(x, g_ref[...]).astype(BF16)
        acc_scr[...] = x

    up = jnp.maximum(_dot(h_scr[...], wup_ref[...]), 0.0)
    acc_scr[...] += _dot((up * up).astype(BF16), wdn_ref[...])

    @pl.when(f == pl.num_programs(1) - 1)
    def _():
        o_ref[...] = acc_scr[...]


def _ffn(x2d, g, wup, wdn, tm, tf):
    m = x2d.shape[0]
    return pl.pallas_call(
        _ffn_kernel,
        grid=(m // tm, D_FF // tf),
        in_specs=[
            pl.BlockSpec((tm, D_MODEL), lambda i, f: (i, 0)),
            pl.BlockSpec((1, D_MODEL), lambda i, f: (0, 0)),
            pl.BlockSpec((D_MODEL, tf), lambda i, f: (0, f)),
            pl.BlockSpec((tf, D_MODEL), lambda i, f: (f, 0)),
        ],
        out_specs=pl.BlockSpec((tm, D_MODEL), lambda i, f: (i, 0)),
        out_shape=jax.ShapeDtypeStruct((m, D_MODEL), F32),
        scratch_shapes=[pltpu.VMEM((tm, D_MODEL), BF16), pltpu.VMEM((tm, D_MODEL), F32)],
        compiler_params=_params(("parallel", "arbitrary")),
        name="ffn",
    )(x2d, g, wup, wdn)


def _tri(keys):
    return (jnp.arange(keys)[:, None] > jnp.arange(keys)[None, :]).astype(BF16)


def _rotary_tables(start, seq):
    half = RET_HEAD_DIM // 2
    pos = start + jnp.arange(seq, dtype=jnp.int32)
    inv_freq = ROPE_BASE ** (-jnp.arange(half, dtype=F32) / half)
    ang = pos.astype(F32)[:, None] * inv_freq[None, :]
    cos = jnp.cos(ang)
    sin = jnp.sin(ang)
    return jnp.concatenate([cos, cos], axis=1), jnp.concatenate([-sin, sin], axis=1)


def _decay_tables(chunk):
    log_g = jnp.log1p(-jnp.exp2(-5.0 - jnp.arange(RET_HEADS, dtype=F32)))
    i = jnp.arange(chunk, dtype=F32)
    diff = i[:, None] - i[None, :]
    dmask = jnp.where(diff[None] >= 0, jnp.exp(jnp.maximum(diff, 0.0)[None] * log_g[:, None, None]), 0.0)
    xi = jnp.exp((i[None, :] + 1.0) * log_g[:, None])
    zeta = jnp.exp((chunk - 1.0 - i)[None, :] * log_g[:, None])
    gch = jnp.exp(chunk * log_g)
    hd = RET_HEAD_DIM
    bc = lambda t: jnp.broadcast_to(t[:, :, None], (RET_HEADS, chunk, hd))
    return dmask, bc(xi), bc(zeta), jnp.broadcast_to(gch[:, None, None], (RET_HEADS, 1, hd))


def _ssm_tables(prep, t_len):
    assert t_len in (SSM_CHUNK, SSM_CHUNK // 2)
    toep, bpre, bpim, cpre, cpim, lam = prep
    tn = t_len * SSM_GROUP
    off = (SSM_CHUNK - t_len) * SSM_GROUP
    bp = jnp.concatenate([bpre, bpim], axis=-1)[:, off:, :]
    bps = jnp.concatenate([bpim, bpre], axis=-1)[:, off:, :]
    cp = jnp.concatenate([cpre, cpim], axis=-1)[:, :tn, :]
    r = 0 if t_len == SSM_CHUNK else 2
    lr, li = lam[:, r:r + 1, :], lam[:, r + 1:r + 2, :]
    rows = jnp.concatenate([jnp.concatenate([lr, lr], -1), jnp.concatenate([-li, li], -1)], axis=1)
    lam8 = jnp.concatenate([rows, jnp.zeros((rows.shape[0], 6, rows.shape[2]), F32)], axis=1)
    return toep[:, :tn, :tn], bp, bps, cp, lam8


def _layer(x3d, past, x0_re, x0_im, s0, lw, prep, consts, layer):
    bsz, seq, _ = x3d.shape
    m = bsz * seq
    x2d = x3d.reshape(m, D_MODEL)
    tm_big = min(1024, m)

    big, u4, k2d, v2d = _in_proj(x2d, lw["gmix"], lw["w_in"], lw["qg"], lw["kg"], consts["bd"], tm_big)

    t_len = min(SSM_CHUNK, seq)
    nchunk = seq // t_len
    n_seq = bsz if nchunk == 1 else 1
    nblk = bsz // n_seq
    g = SSM_GROUPS
    nslab = BLK // LANES
    x0 = jnp.stack([jnp.concatenate([x0_re, x0_im], axis=-1), jnp.concatenate([x0_im, x0_re], axis=-1)], axis=2)
    x0 = x0.reshape(nblk, n_seq, 2 * g, 2 * SSM_STATE).transpose(0, 2, 1, 3)
    toep, bp, bps, cp, lam8 = _ssm_tables(prep, t_len)
    dfull = jnp.tile(lw["ssm_d"], (1, t_len))[:, None, :]
    y4, xf = _ssm(u4.reshape(nslab, nblk, m // nblk, LANES), x0, toep, bp, bps, cp, lam8, dfull, n_seq, nchunk, t_len)
    y4 = y4.reshape(nslab, m, LANES)
    xf = xf.transpose(0, 2, 1, 3).reshape(bsz, g, 2 * SSM_STATE)
    ssm_re, ssm_im = xf[..., :SSM_STATE], xf[..., SSM_STATE:]

    if past is None:
        ob = _sb_prompt(lw["sb_bias"], big, k2d, v2d, _tri(min(SB_TILE, seq)), bsz, seq)
        start = 0
    else:
        cache_kt, cache_vt, page_table = past
        ob = _sb_paged(page_table, lw["sb_bias"], big, k2d, v2d, consts["tri_page"], cache_kt, cache_vt, layer, seq)
        start = page_table.shape[1] * PAGE_SIZE

    cosf, sinf = _rotary_tables(start, seq)
    dmask, xi, zeta, gch = _decay_tables(min(RET_CHUNK, seq))
    oc, ret_state = _retention(big, cosf, sinf, dmask, xi, zeta, gch, s0, lw["ret_gain"], bsz, seq)

    x1 = _merge(x2d, lw["gmix"], y4, ob, oc, lw["wgate"], lw["bgate"], lw["wglu"], lw["wbs"], lw["wbb"], lw["wbr"],
                lw["wo"], min(256, m))
    x2 = _ffn(x1, lw["gffn"], lw["wup"], lw["wdn"], tm_big, 1024)
    shape_kv = (bsz, seq, SB_HEADS, SB_HEAD_DIM)
    return x2.reshape(bsz, seq, D_MODEL), k2d.reshape(shape_kv), v2d.reshape(shape_kv), ssm_re, ssm_im, ret_state


def kernel(x_prompt, x_sample, cache_k, cache_v, state_ssm_re, state_ssm_im, state_ret, page_table, norm_mix, w_in, sb_q_norm, sb_k_norm, sb_logit_bias, ssm_lambda_re, ssm_lambda_im, ssm_log_dt, ssm_b_re, ssm_b_im, ssm_c_re, ssm_c_im, ssm_d, ssm_w_glu, ret_norm, w_branch_ssm, w_branch_sb, w_branch_ret, w_gate, b_gate, w_o, norm_ffn, w_ff_up, w_ff_down):
    depth = w_in.shape[0]
    n_prompt = x_prompt.shape[0]
    n_phys = cache_k.shape[1]
    cache_kt = cache_k.transpose(0, 1, 3, 4, 2).reshape(depth, n_phys, BLK, PAGE_SIZE)
    cache_vt = cache_v.transpose(0, 1, 3, 4, 2).reshape(depth, n_phys, BLK, PAGE_SIZE)
    lane_head = jnp.arange(BLK) // SB_HEAD_DIM
    consts = {
        "bd": jnp.where(lane_head[:, None] == lane_head[None, :], 1.0 / SB_HEAD_DIM, 0.0).astype(BF16),
        "tri_page": _tri(PAGE_SIZE),
    }
    zeros_ssm = jnp.zeros((n_prompt, SSM_GROUPS, SSM_STATE), F32)
    zeros_ret = jnp.zeros((n_prompt, RET_HEADS, RET_HEAD_DIM, RET_HEAD_DIM), F32)

    yp, ys = x_prompt, x_sample
    outs_p, outs_s = [], []
    for l in range(depth):
        lw = {
            "gmix": norm_mix[l][None, :],
            "w_in": w_in[l].astype(BF16),
            "qg": jnp.tile(sb_q_norm[l], SB_HEADS)[None, :],
            "kg": jnp.tile(sb_k_norm[l], SB_HEADS)[None, :],
            "sb_bias": sb_logit_bias[l],
            "ssm_d": ssm_d[l],
            "ret_gain": ret_norm[l][None, :],
            "wgate": w_gate[l].astype(BF16),
            "bgate": b_gate[l][None, :],
            "wglu": ssm_w_glu[l].astype(BF16),
            "wbs": w_branch_ssm[l].astype(BF16),
            "wbb": w_branch_sb[l].astype(BF16),
            "wbr": w_branch_ret[l].astype(BF16),
            "wo": w_o[l].astype(BF16),
            "gffn": norm_ffn[l][None, :],
            "wup": w_ff_up[l].astype(BF16),
            "wdn": w_ff_down[l].astype(BF16),
        }
        prep = _ssm_prep(ssm_lambda_re[l][:, None, :], ssm_lambda_im[l][:, None, :], ssm_log_dt[l][:, None, None],
                         ssm_b_re[l].transpose(0, 2, 1), ssm_b_im[l].transpose(0, 2, 1), ssm_c_re[l], ssm_c_im[l])
        rp = _layer(yp, None, zeros_ssm, zeros_ssm, zeros_ret, lw, prep, consts, l)
        rs = _layer(ys, (cache_kt, cache_vt, page_table), state_ssm_re[l], state_ssm_im[l], state_ret[l],
                    lw, prep, consts, l)
        yp, ys = rp[0], rs[0]
        outs_p.append(rp[1:])
        outs_s.append(rs[1:])
    stack = lambda outs, i: jnp.stack([o[i] for o in outs])
    return (yp, ys) + tuple(stack(outs_p, i) for i in range(5)) + tuple(stack(outs_s, i) for i in range(5))
```

```python
import functools

import jax
import jax.numpy as jnp
from jax import lax
from jax.experimental import pallas as pl
from jax.experimental.pallas import tpu as pltpu

F32 = jnp.float32
BF16 = jnp.bfloat16

EPS = 1e-6
D_MODEL = 1024
SSM_GROUP = 16
SSM_GROUPS = 32
SSM_STATE = 64
SSM_CHUNK = 16
SB_HEAD_DIM = 64
SB_HEADS = 8
RET_HEAD_DIM = 128
RET_HEADS = 4
RET_CHUNK = 128
ROPE_BASE = 10000.0
D_FF = 4 * D_MODEL
PAGE_SIZE = 128
BLK = 512
IN_WIDTH = 8 * BLK
GATE_WIDTH = 3 * D_MODEL
LANES = 128
SB_TILE = 256
PAGES_PER_STEP = 8
PAGE_SLOTS = 3
VMEM_LIMIT = 56 * 1024 * 1024

_NT = (((1,), (1,)), ((), ()))
_TN = (((0,), (0,)), ((), ()))


def _dot(a, b):
    return jnp.dot(a, b, preferred_element_type=F32)


def _dot_nt(a, b):
    return lax.dot_general(a, b, _NT, preferred_element_type=F32)


def _split2(x):
    hi = x.astype(BF16)
    lo = (x - hi.astype(F32)).astype(BF16)
    return hi, lo


def _dot_x3(a, b, nt=False):
    d = _dot_nt if nt else _dot
    ah, al = _split2(a)
    bh, bl = _split2(b)
    return d(ah, bh) + (d(ah, bl) + d(al, bh))


def _div_pow2(x, d):
    assert d & (d - 1) == 0
    return lax.shift_right_logical(x, jnp.int32(d.bit_length() - 1))


def _softplus(z):
    return jnp.maximum(z, 0.0) + jnp.log(1.0 + jnp.exp(-jnp.abs(z)))


def _rms_norm(x, g):
    ms = jnp.mean(x * x, axis=-1, keepdims=True)
    return x * lax.rsqrt(ms + EPS) * g


def _params(sem):
    return pltpu.CompilerParams(dimension_semantics=sem, vmem_limit_bytes=VMEM_LIMIT)


def _in_proj_kernel(x_ref, gmix_ref, w_ref, qg_ref, kg_ref, bd_ref, u4_ref, q_ref, k_ref, v_ref, ret_ref, h_scr):
    j = pl.program_id(1)

    @pl.when(j == 0)
    def _():
        h_scr[...] = _rms_norm(x_ref[...], gmix_ref[...]).astype(BF16)

    acc = _dot(h_scr[...], w_ref[...])

    def head_norm(a, g_ref):
        hi, lo = _split2(a * a)
        msq = _dot(hi, bd_ref[...]) + _dot(lo, bd_ref[...])
        return a * lax.rsqrt(msq + EPS) * g_ref[...]

    @pl.when(j == 0)
    def _():
        for s in range(BLK // LANES):
            u4_ref[s] = acc[:, s * LANES:(s + 1) * LANES]
        q_ref[...] = head_norm(acc[:, BLK:], qg_ref)

    @pl.when(j == 1)
    def _():
        k_ref[...] = head_norm(acc[:, :BLK], kg_ref)
        v_ref[...] = acc[:, BLK:]

    @pl.when(j >= 2)
    def _():
        ret_ref[...] = acc


def _in_proj(x2d, gmix, w_in, qg, kg, bd, tm):
    m = x2d.shape[0]
    sds = jax.ShapeDtypeStruct
    nslab = BLK // LANES
    wide = 2 * BLK

    def resident(width):
        return pl.BlockSpec((tm, width), lambda i, j: (i, 0))

    return pl.pallas_call(
        _in_proj_kernel,
        grid=(m // tm, IN_WIDTH // wide),
        in_specs=[
            pl.BlockSpec((tm, D_MODEL), lambda i, j: (i, 0)),
            pl.BlockSpec((1, D_MODEL), lambda i, j: (0, 0)),
            pl.BlockSpec((D_MODEL, wide), lambda i, j: (0, j)),
            pl.BlockSpec((1, BLK), lambda i, j: (0, 0)),
            pl.BlockSpec((1, BLK), lambda i, j: (0, 0)),
            pl.BlockSpec((BLK, BLK), lambda i, j: (0, 0)),
        ],
        out_specs=[
            pl.BlockSpec((nslab, tm, LANES), lambda i, j: (0, i, 0)),
            resident(BLK), resident(BLK), resident(BLK),
            pl.BlockSpec((tm, wide), lambda i, j: (i, jnp.maximum(j - 2, 0))),
        ],
        out_shape=[sds((nslab, m, LANES), F32), sds((m, BLK), F32), sds((m, BLK), F32), sds((m, BLK), F32),
                   sds((m, IN_WIDTH // 2), F32)],
        scratch_shapes=[pltpu.VMEM((tm, D_MODEL), BF16)],
        compiler_params=_params(("parallel", "arbitrary")),
        name="in_proj",
    )(x2d, gmix, w_in, qg, kg, bd)


def _ssm_prep_kernel(lre_ref, lim_ref, ldt_ref, btre_ref, btim_ref, cre_ref, cim_ref,
                     toep_ref, bpre_ref, bpim_ref, cpre_ref, cpim_ref, lam_ref):
    t_len = SSM_CHUNK
    lre = lre_ref[...]
    lim = lim_ref[...]
    dt = jnp.exp(ldt_ref[...])
    a = lre * dt
    b = lim * dt
    ea = jnp.exp(a)
    lbr = ea * jnp.cos(b)
    lbi = ea * jnp.sin(b)
    den = lre * lre + lim * lim
    xr = lbr - 1.0
    cr = (xr * lre + lbi * lim) / den
    ci = (lbi * lre - xr * lim) / den
    btre = btre_ref[...]
    btim = btim_ref[...]
    bbr = cr * btre - ci * btim
    bbi = cr * btim + ci * btre
    cre = cre_ref[...]
    cim = cim_ref[...]

    one = jnp.ones_like(lre)
    zero = jnp.zeros_like(lre)
    pows = [(one, zero)]
    for _ in range(t_len):
        pr, pi = pows[-1]
        pows.append((pr * lbr - pi * lbi, pr * lbi + pi * lbr))

    def cmul(xre, xim, p):
        return xre * p[0] - xim * p[1], xre * p[1] + xim * p[0]

    def stack(parts):
        return jnp.concatenate(parts, axis=0)

    a_parts = [cmul(cre, cim, pows[t]) for t in range(t_len)]
    a1_parts = [cmul(cre, cim, pows[t + 1]) for t in range(t_len)]
    h_parts = [cmul(bbr, bbi, pows[t_len - 1 - s]) for s in range(t_len)]
    a_re = stack([p[0] for p in a_parts])
    a_im = stack([p[1] for p in a_parts])

    tn = t_len * SSM_GROUP
    r0 = _dot_x3(bbr, a_re, nt=True) - _dot_x3(bbi, a_im, nt=True)
    lane = lax.broadcasted_iota(jnp.int32, (SSM_GROUP, tn), 1)
    blocks = [r0]
    for s in range(1, t_len):
        blocks.append(jnp.where(lane >= s * SSM_GROUP, pltpu.roll(r0, s * SSM_GROUP, 1), 0.0))
    toep_ref[...] = stack(blocks)
    bpre_ref[...] = stack([p[0] for p in h_parts])
    bpim_ref[...] = stack([p[1] for p in h_parts])
    cpre_ref[...] = stack([p[0] for p in a1_parts])
    cpim_ref[...] = -stack([p[1] for p in a1_parts])
    half = t_len // 2
    lam_ref[...] = stack([pows[t_len][0], pows[t_len][1], pows[half][0], pows[half][1],
                          zero, zero, zero, zero])


def _ssm_prep(lre, lim, ldt, btre, btim, cre, cim):
    g = lre.shape[0]
    n, p = SSM_GROUP, SSM_STATE
    tn = SSM_CHUNK * n
    sds = jax.ShapeDtypeStruct

    def spec(r, c):
        return pl.BlockSpec((None, r, c), lambda i: (i, 0, 0))

    return pl.pallas_call(
        _ssm_prep_kernel,
        grid=(g,),
        in_specs=[spec(1, p), spec(1, p), spec(1, 1), spec(n, p), spec(n, p), spec(n, p), spec(n, p)],
        out_specs=[spec(tn, tn), spec(tn, p), spec(tn, p), spec(tn, p), spec(tn, p), spec(8, p)],
        out_shape=[sds((g, tn, tn), F32), sds((g, tn, p), F32), sds((g, tn, p), F32),
                   sds((g, tn, p), F32), sds((g, tn, p), F32), sds((g, 8, p), F32)],
        compiler_params=_params(("parallel",)),
        name="ssm_prep",
    )(lre, lim, ldt, btre, btim, cre, cim)


def _ssm_kernel(u_ref, x0_ref, toep_ref, bp_ref, bps_ref, cp_ref, lam_ref, d_ref, y_ref, xf_ref,
                w_scr, ws_scr, xall_scr, *, n_seq, n_chunks, t_len):
    assert n_seq == 1 or n_chunks == 1
    n = SSM_GROUP
    per = LANES // n
    halves = t_len // per
    lane_run = _div_pow2(lax.broadcasted_iota(jnp.int32, (1, LANES), 1), n)
    rows = n_seq * n_chunks
    pieces = [u_ref[pl.ds(s, rows, stride=t_len), :] for s in range(t_len)]

    def transpose_runs(vs):
        vs = list(vs)
        d = per // 2
        while d:
            low = (lane_run & d) == 0
            for i in range(per):
                if i & d == 0:
                    a, b = vs[i], vs[i + d]
                    vs[i] = jnp.where(low, a, pltpu.roll(b, d * n, 1))
                    vs[i + d] = jnp.where(low, pltpu.roll(a, LANES - d * n, 1), b)
            d //= 2
        return vs

    u_halves = [transpose_runs(pieces[k * per:(k + 1) * per]) for k in range(halves)]
    ugs = [u_halves[0][gl] if halves == 1 else jnp.concatenate([h[gl] for h in u_halves], axis=1)
           for gl in range(per)]

    for gl in range(per):
        w_scr[gl] = _dot_x3(ugs[gl], bp_ref[gl])
    for gl in range(per):
        ws_scr[gl] = _dot_x3(ugs[gl], bps_ref[gl])

    l1 = [jnp.broadcast_to(lam_ref[gl, 0:1, :], (n_seq, 2 * SSM_STATE)) for gl in range(per)]
    l2 = [jnp.broadcast_to(lam_ref[gl, 1:2, :], (n_seq, 2 * SSM_STATE)) for gl in range(per)]

    def step(rows, carry):
        out = []
        for gl in range(per):
            x, xs = carry[2 * gl], carry[2 * gl + 1]
            xall_scr[gl, rows, :] = x
            out.append(x * l1[gl] + xs * l2[gl] + w_scr[gl, rows, :])
            out.append(xs * l1[gl] - x * l2[gl] + ws_scr[gl, rows, :])
        return tuple(out)

    init = tuple(x0_ref[i] for i in range(2 * per))
    if n_chunks == 1:
        fin = step(slice(None), init)
    else:
        fin = lax.fori_loop(0, n_chunks, lambda c, carry: step(pl.ds(c, 1), carry), init)

    ys = []
    for gl in range(per):
        xf_ref[gl] = fin[2 * gl]
        y = (_dot_x3(ugs[gl], toep_ref[gl]) + _dot_x3(xall_scr[gl], cp_ref[gl], nt=True)
             + ugs[gl] * d_ref[gl])
        y = jax.nn.gelu(y)
        ys.append([y[:, k * LANES:(k + 1) * LANES] for k in range(halves)])
    for k in range(halves):
        back = transpose_runs([ys[gl][k] for gl in range(per)])
        for j in range(per):
            y_ref[pl.ds(k * per + j, rows, stride=t_len), :] = back[j]


def _ssm(u4, x0, toep, bp, bps, cp, lam, dfull, n_seq, n_chunks, t_len):
    nslab, nblk, tokens, width = u4.shape
    rows = n_seq * n_chunks
    per = LANES // SSM_GROUP
    p2 = 2 * SSM_STATE
    tn = t_len * SSM_GROUP
    sds = jax.ShapeDtypeStruct

    def table(*shape):
        return pl.BlockSpec((per,) + shape, lambda s, b: (s,) + (0,) * len(shape))

    kern = functools.partial(_ssm_kernel, n_seq=n_seq, n_chunks=n_chunks, t_len=t_len)
    return pl.pallas_call(
        kern,
        grid=(nslab, nblk),
        in_specs=[pl.BlockSpec((None, None, tokens, width), lambda s, b: (s, b, 0, 0)),
                  pl.BlockSpec((None, 2 * per, n_seq, p2), lambda s, b: (b, s, 0, 0)),
                  table(tn, tn), table(tn, p2), table(tn, p2), table(tn, p2), table(8, p2), table(1, tn)],
        out_specs=[pl.BlockSpec((None, None, tokens, width), lambda s, b: (s, b, 0, 0)),
                   pl.BlockSpec((None, per, n_seq, p2), lambda s, b: (b, s, 0, 0))],
        out_shape=[sds(u4.shape, F32), sds((nblk, nslab * per, n_seq, p2), F32)],
        scratch_shapes=[pltpu.VMEM((per, rows, p2), F32)] * 3,
        compiler_params=_params(("parallel", "parallel")),
        name="ssm",
    )(u4, x0, toep, bp, bps, cp, lam, dfull)


def _sb_logs(z, mask):
    sp = _softplus(z)
    lsig = z - sp
    if mask is not None:
        sp = jnp.where(mask, sp, 0.0)
    return lsig, sp, jnp.sum(sp, axis=1, keepdims=True)


def _sb_suffix(sp, tri, terms=2):
    if terms == 1:
        return _dot(sp.astype(BF16), tri)
    hi, lo = _split2(sp)
    return _dot(hi, tri) + _dot(lo, tri)


def _sb_weights(lsig, cs, carry, mask):
    att = jnp.exp(lsig - (cs + carry))
    if mask is not None:
        att = jnp.where(mask, att, 0.0)
    return att.astype(BF16)


def _sb_prompt_kernel(bias_ref, q_ref, k_ref, v_ref, tri_ref, o_ref, carry_scr, acc_scr, *, tile):
    hp = pl.program_id(1)
    qi = pl.program_id(2)
    q = q_ref[...] * (SB_HEAD_DIM ** -0.5)
    lane = lax.broadcasted_iota(jnp.int32, (1, LANES), 1)
    head_lanes = (lane < SB_HEAD_DIM, lane >= SB_HEAD_DIM)
    qh = [jnp.where(hl, q, 0.0).astype(BF16) for hl in head_lanes]
    bias = [bias_ref[hp * 2], bias_ref[hp * 2 + 1]]
    row = lax.broadcasted_iota(jnp.int32, (tile, tile), 0)
    col = lax.broadcasted_iota(jnp.int32, (tile, tile), 1)
    causal = col < row
    carry_scr[...] = jnp.zeros_like(carry_scr)
    acc_scr[...] = jnp.zeros_like(acc_scr)

    def blocks(kjs, mask):
        tri = tri_ref[...]
        carry = [carry_scr[0], carry_scr[1]]
        chains = [(i, hh) for i in range(len(kjs)) for hh in range(2)]
        k0s = [pl.multiple_of(kj * tile, tile) for kj in kjs]
        kbs = [k_ref[pl.ds(k0, tile), :].astype(BF16) for k0 in k0s]
        logs = [_sb_logs(_dot_nt(qh[hh], kbs[i]) + bias[hh], mask) for i, hh in chains]
        css = [_sb_suffix(lg[1], tri, terms=1) for lg in logs]
        atts = []
        for (i, hh), lg, cs in zip(chains, logs, css):
            atts.append(_sb_weights(lg[0], cs, jnp.concatenate([carry[hh]] * (tile // LANES), axis=1), mask))
            carry[hh] = carry[hh] + lg[2]
        out = None
        for (i, hh), att in zip(chains, atts):
            v = v_ref[pl.ds(k0s[i], tile), :]
            o = _dot(att, jnp.where(head_lanes[hh], v, 0.0).astype(BF16))
            out = o if out is None else out + o
        carry_scr[0] = carry[0]
        carry_scr[1] = carry[1]
        acc_scr[...] += out

    blocks([qi], causal)

    def body(t, _):
        blocks([qi - 1 - 2 * t, qi - 2 - 2 * t], None)
        return 0

    lax.fori_loop(0, lax.shift_right_logical(qi, 1), body, 0)

    @pl.when((qi & 1) == 1)
    def _():
        blocks([0], None)

    o_ref[...] = acc_scr[...]


def _sb_prompt(bias, big, k2d, v2d, tri, bsz, seq):
    tile = min(SB_TILE, seq)
    nq = seq // tile
    m = bsz * seq
    kern = functools.partial(_sb_prompt_kernel, tile=tile)
    q_col0 = 0
    return pl.pallas_call(
        kern,
        grid=(bsz, SB_HEADS // 2, nq),
        in_specs=[
            pl.BlockSpec(memory_space=pltpu.SMEM),
            pl.BlockSpec((tile, LANES), lambda b, h, i: (b * nq + i, q_col0 + h)),
            pl.BlockSpec((seq, LANES), lambda b, h, i: (b, h)),
            pl.BlockSpec((seq, LANES), lambda b, h, i: (b, h)),
            pl.BlockSpec((tile, tile), lambda b, h, i: (0, 0)),
        ],
        out_specs=pl.BlockSpec((tile, LANES), lambda b, h, i: (b * nq + i, h)),
        out_shape=jax.ShapeDtypeStruct((m, BLK), F32),
        scratch_shapes=[pltpu.VMEM((2, tile, LANES), F32), pltpu.VMEM((tile, LANES), F32)],
        compiler_params=_params(("parallel", "parallel", "arbitrary")),
        name="sb_prompt",
    )(bias, big, k2d, v2d, tri)


def _sb_paged_kernel(pt_ref, bias_ref, q_ref, kn_ref, vn_ref, tri_ref, *rest, n_q, layer, n_seq, steps, n_pages):
    npg = PAGES_PER_STEP
    ck_hbm, cv_hbm, o_ref, kbuf, vbuf, sems, qbd_scr, bias_scr, carry_scr, acc_scr = rest
    rows = SB_HEADS * n_q
    s = pl.program_id(1)
    t = pl.program_id(0) * steps + s
    total = n_seq * steps

    def page_copies(tt):
        slot = lax.rem(tt, PAGE_SLOTS)
        bb = lax.div(tt, steps)
        ss = tt - bb * steps
        copies = []
        for i in range(npg):
            page = pt_ref[bb, n_pages - 1 - (ss * npg + i)]
            copies.append(pltpu.make_async_copy(ck_hbm.at[layer, page], kbuf.at[slot, i], sems.at[slot, 0]))
            copies.append(pltpu.make_async_copy(cv_hbm.at[layer, page], vbuf.at[slot, i], sems.at[slot, 1]))
        return copies

    @pl.when(t == 0)
    def _():
        for d in range(min(PAGE_SLOTS - 1, total)):
            for c in page_copies(jnp.int32(d)):
                c.start()

    @pl.when(t + (PAGE_SLOTS - 1) < total)
    def _():
        for c in page_copies(t + (PAGE_SLOTS - 1)):
            c.start()

    @pl.when(s == 0)
    def _():
        row_head = _div_pow2(lax.broadcasted_iota(jnp.int32, (rows, BLK), 0), n_q)
        lane_head = _div_pow2(lax.broadcasted_iota(jnp.int32, (rows, BLK), 1), SB_HEAD_DIM)
        q_all = jnp.concatenate([q_ref[...]] * SB_HEADS, axis=0) * (SB_HEAD_DIM ** -0.5)
        qbd = jnp.where(row_head == lane_head, q_all, 0.0).astype(BF16)
        qbd_scr[...] = qbd
        rh = _div_pow2(lax.broadcasted_iota(jnp.int32, (rows, LANES), 0), n_q)
        bt = jnp.zeros((rows, LANES), F32)
        for h in range(SB_HEADS):
            bt = jnp.where(rh == h, bias_ref[h], bt)
        bias_scr[...] = bt
        qpos = lax.broadcasted_iota(jnp.int32, (rows, PAGE_SIZE), 0) & (n_q - 1)
        kpos = lax.broadcasted_iota(jnp.int32, (rows, PAGE_SIZE), 1)
        mask = kpos < qpos
        pad = jnp.zeros((PAGE_SIZE - n_q, BLK), F32)
        kn = jnp.concatenate([kn_ref[...], pad], axis=0).astype(BF16)
        vn = jnp.concatenate([vn_ref[...], pad], axis=0).astype(BF16)
        z = _dot_nt(qbd, kn) + bt
        lsig, sp, tot = _sb_logs(z, mask)
        att = _sb_weights(lsig, _sb_suffix(sp, tri_ref[...]), 0.0, mask)
        acc_scr[...] = _dot(att, vn)
        carry_scr[...] = jnp.broadcast_to(tot, (rows, LANES))

    for c in page_copies(t):
        c.wait()
    slot = lax.rem(t, PAGE_SLOTS)
    k_pages = [kbuf[slot, i] for i in range(npg)]
    v_pages = [vbuf[slot, i] for i in range(npg)]

    kcat = jnp.concatenate([kp.astype(BF16) for kp in k_pages], axis=1)
    z = _dot(qbd_scr[...], kcat) + jnp.concatenate([bias_scr[...]] * npg, axis=1)
    sp = _softplus(z)
    lsig = z - sp
    sp_rows = jnp.concatenate([sp[:, i * PAGE_SIZE:(i + 1) * PAGE_SIZE] for i in range(npg)], axis=0)
    cs_rows = _sb_suffix(sp_rows, tri_ref[...])
    tot_rows = jnp.sum(sp_rows, axis=1, keepdims=True)
    carry = carry_scr[...]
    after = []
    for i in range(npg):
        after.append(cs_rows[i * rows:(i + 1) * rows] + carry)
        carry = carry + tot_rows[i * rows:(i + 1) * rows]
    carry_scr[...] = carry
    att = jnp.exp(lsig - jnp.concatenate(after, axis=1)).astype(BF16)
    vcat = jnp.concatenate([vp.astype(BF16) for vp in v_pages], axis=1)
    acc_scr[...] += _dot_nt(att, vcat)

    @pl.when(s == pl.num_programs(1) - 1)
    def _():
        lane_head = _div_pow2(lax.broadcasted_iota(jnp.int32, (n_q, BLK), 1), SB_HEAD_DIM)
        out = acc_scr[0:n_q, :]
        for h in range(1, SB_HEADS):
            out = jnp.where(lane_head == h, acc_scr[h * n_q:(h + 1) * n_q, :], out)
        o_ref[...] = out


def _sb_paged(page_table, bias, big, k_new, v_new, tri, cache_kt, cache_vt, layer, n_q):
    n_seq, n_pages = page_table.shape
    npg = PAGES_PER_STEP
    steps = n_pages // npg
    rows = SB_HEADS * n_q
    q_col = 0
    ring = (PAGE_SLOTS, npg, BLK, PAGE_SIZE)

    grid_spec = pltpu.PrefetchScalarGridSpec(
        num_scalar_prefetch=1,
        grid=(n_seq, steps),
        in_specs=[
            pl.BlockSpec(memory_space=pltpu.SMEM),
            pl.BlockSpec((n_q, BLK), lambda b, s, pt: (b, q_col)),
            pl.BlockSpec((n_q, BLK), lambda b, s, pt: (b, 0)),
            pl.BlockSpec((n_q, BLK), lambda b, s, pt: (b, 0)),
            pl.BlockSpec((PAGE_SIZE, PAGE_SIZE), lambda b, s, pt: (0, 0)),
            pl.BlockSpec(memory_space=pl.ANY),
            pl.BlockSpec(memory_space=pl.ANY),
        ],
        out_specs=pl.BlockSpec((n_q, BLK), lambda b, s, pt: (b, 0)),
        scratch_shapes=[pltpu.VMEM(ring, F32), pltpu.VMEM(ring, F32), pltpu.SemaphoreType.DMA((PAGE_SLOTS, 2)),
                        pltpu.VMEM((rows, BLK), BF16), pltpu.VMEM((rows, LANES), F32),
                        pltpu.VMEM((rows, LANES), F32), pltpu.VMEM((rows, BLK), F32)],
    )
    kern = functools.partial(_sb_paged_kernel, n_q=n_q, layer=layer, n_seq=n_seq, steps=steps, n_pages=n_pages)
    return pl.pallas_call(
        kern,
        grid_spec=grid_spec,
        out_shape=jax.ShapeDtypeStruct((n_seq * n_q, BLK), F32),
        compiler_params=_params(("arbitrary", "arbitrary")),
        name="sb_paged",
    )(page_table, bias, big, k_new, v_new, tri, cache_kt, cache_vt)


def _ret_kernel(q_ref, k_ref, v_ref, g_ref, cos_ref, sin_ref, dmask_ref, xi_ref, zeta_ref, gch_ref,
                s0_ref, gain_ref, o_ref, sout_ref, s_scr):
    c = pl.program_id(1)

    @pl.when(c == 0)
    def _():
        s_scr[...] = s0_ref[...]

    cos = cos_ref[...]
    sin = sin_ref[...]
    half = RET_HEAD_DIM // 2
    heads = range(RET_HEADS)
    sls = [slice(h * RET_HEAD_DIM, (h + 1) * RET_HEAD_DIM) for h in heads]
    rq_b, rk, v_b = [], [], []
    for h in heads:
        qh = q_ref[:, sls[h]]
        kh = k_ref[:, sls[h]]
        rq_b.append((qh * cos + pltpu.roll(qh, half, 1) * sin).astype(BF16))
        rk.append((kh * cos + pltpu.roll(kh, half, 1) * sin) * (RET_HEAD_DIM ** -0.5))
        v_b.append(v_ref[:, sls[h]].astype(BF16))
    states = [s_scr[h] for h in heads]
    inner = [(_dot_nt(rq_b[h], rk[h].astype(BF16)) * dmask_ref[h]).astype(BF16) for h in heads]
    cross = [_dot(rq_b[h], states[h].astype(BF16)) * xi_ref[h] for h in heads]
    outs = [_dot(inner[h], v_b[h]) + cross[h] for h in heads]
    for h in heads:
        kz = (rk[h] * zeta_ref[h]).astype(BF16)
        s_scr[h] = states[h] * gch_ref[h] + lax.dot_general(kz, v_b[h], _TN, preferred_element_type=F32)
    for h in heads:
        o = outs[h]
        mu = jnp.mean(o, axis=-1, keepdims=True)
        d = o - mu
        var = jnp.mean(d * d, axis=-1, keepdims=True)
        y = d * lax.rsqrt(var + EPS) * gain_ref[:, sls[h]]
        gate = g_ref[:, sls[h]]
        o_ref[:, sls[h]] = gate * jax.nn.sigmoid(gate) * y

    @pl.when(c == pl.num_programs(1) - 1)
    def _():
        sout_ref[...] = s_scr[...]


def _retention(big, cosf, sinf, dmask, xi, zeta, gch, s0, gain, bsz, seq):
    chunk = min(RET_CHUNK, seq)
    nc = seq // chunk
    m = bsz * seq
    hd = RET_HEAD_DIM
    sds = jax.ShapeDtypeStruct

    def col(j):
        return pl.BlockSpec((chunk, BLK), lambda b, c: (b * nc + c, j))

    def const3(shape):
        return pl.BlockSpec(shape, lambda b, c: (0, 0, 0))

    state_spec = pl.BlockSpec((None, RET_HEADS, hd, hd), lambda b, c: (b, 0, 0, 0))
    return pl.pallas_call(
        _ret_kernel,
        grid=(bsz, nc),
        in_specs=[
            col(0), col(1), col(2), col(3),
            pl.BlockSpec((chunk, hd), lambda b, c: (c, 0)),
            pl.BlockSpec((chunk, hd), lambda b, c: (c, 0)),
            const3((RET_HEADS, chunk, chunk)),
            const3((RET_HEADS, chunk, hd)),
            const3((RET_HEADS, chunk, hd)),
            const3((RET_HEADS, 1, hd)),
            state_spec,
            pl.BlockSpec((1, BLK), lambda b, c: (0, 0)),
        ],
        out_specs=[pl.BlockSpec((chunk, BLK), lambda b, c: (b * nc + c, 0)), state_spec],
        out_shape=[sds((m, BLK), F32), sds((bsz, RET_HEADS, hd, hd), F32)],
        scratch_shapes=[pltpu.VMEM((RET_HEADS, hd, hd), F32)],
        compiler_params=_params(("parallel", "arbitrary")),
        name="retention",
    )(big, big, big, big, cosf, sinf, dmask, xi, zeta, gch, s0, gain)


def _merge_kernel(x_ref, gmix_ref, ys_ref, ob_ref, oc_ref, wgate_ref, bgate_ref,
                  wglu_ref, wbs_ref, wbb_ref, wbr_ref, wo_ref, o_ref):
    x = x_ref[...]
    h = _rms_norm(x, gmix_ref[...]).astype(BF16)
    gates = jax.nn.sigmoid(_dot(h, wgate_ref[...]) + bgate_ref[...])
    ys = jnp.concatenate([ys_ref[s] for s in range(BLK // LANES)], axis=1)
    glu = _dot(ys.astype(BF16), wglu_ref[...])
    out_a = glu[:, :BLK] * jax.nn.sigmoid(glu[:, BLK:])
    d = D_MODEL
    merged = (gates[:, :d] * _dot(out_a.astype(BF16), wbs_ref[...])
              + gates[:, d:2 * d] * _dot(ob_ref[...].astype(BF16), wbb_ref[...])
              + gates[:, 2 * d:] * _dot(oc_ref[...].astype(BF16), wbr_ref[...]))
    o_ref[...] = x + _dot(merged.astype(BF16), wo_ref[...])


def _merge(x2d, gmix, ys, ob, oc, wgate, bgate, wglu, wbs, wbb, wbr, wo, tm):
    m = x2d.shape[0]

    def rows(width):
        return pl.BlockSpec((tm, width), lambda i: (i, 0))

    def whole(shape):
        return pl.BlockSpec(shape, lambda i: (0, 0), pipeline_mode=pl.Buffered(1))

    return pl.pallas_call(
        _merge_kernel,
        grid=(m // tm,),
        in_specs=[rows(D_MODEL), whole((1, D_MODEL)),
                  pl.BlockSpec((BLK // LANES, tm, LANES), lambda i: (0, i, 0)), rows(BLK), rows(BLK),
                  whole((D_MODEL, GATE_WIDTH)), whole((1, GATE_WIDTH)),
                  whole((BLK, 2 * BLK)), whole((BLK, D_MODEL)), whole((BLK, D_MODEL)), whole((BLK, D_MODEL)),
                  whole((D_MODEL, D_MODEL))],
        out_specs=rows(D_MODEL),
        out_shape=jax.ShapeDtypeStruct((m, D_MODEL), F32),
        compiler_params=_params(("parallel",)),
        name="merge",
    )(x2d, gmix, ys, ob, oc, wgate, bgate, wglu, wbs, wbb, wbr, wo)


def _ffn_kernel(x_ref, g_ref, wup_ref, wdn_ref, o_ref, h_scr, acc_scr):
    f = pl.program_id(1)

    @pl.when(f == 0)
    def _():
        x = x_ref[...]
        h_scr[...] = _rms_norm(x, g_ref[...]).astype(BF16)
        acc_scr[...] = x

    up = jnp.maximum(_dot(h_scr[...], wup_ref[...]), 0.0)
    acc_scr[...] += _dot((up * up).astype(BF16), wdn_ref[...])

    @pl.when(f == pl.num_programs(1) - 1)
    def _():
        o_ref[...] = acc_scr[...]


def _ffn(x2d, g, wup, wdn, tm, tf):
    m = x2d.shape[0]
    return pl.pallas_call(
        _ffn_kernel,
        grid=(m // tm, D_FF // tf),
        in_specs=[
            pl.BlockSpec((tm, D_MODEL), lambda i, f: (i, 0)),
            pl.BlockSpec((1, D_MODEL), lambda i, f: (0, 0)),
            pl.BlockSpec((D_MODEL, tf), lambda i, f: (0, f)),
            pl.BlockSpec((tf, D_MODEL), lambda i, f: (f, 0)),
        ],
        out_specs=pl.BlockSpec((tm, D_MODEL), lambda i, f: (i, 0)),
        out_shape=jax.ShapeDtypeStruct((m, D_MODEL), F32),
        scratch_shapes=[pltpu.VMEM((tm, D_MODEL), BF16), pltpu.VMEM((tm, D_MODEL), F32)],
        compiler_params=_params(("parallel", "arbitrary")),
        name="ffn",
    )(x2d, g, wup, wdn)


def _tri(keys):
    return (jnp.arange(keys)[:, None] > jnp.arange(keys)[None, :]).astype(BF16)


def _rotary_tables(start, seq):
    half = RET_HEAD_DIM // 2
    pos = start + jnp.arange(seq, dtype=jnp.int32)
    inv_freq = ROPE_BASE ** (-jnp.arange(half, dtype=F32) / half)
    ang = pos.astype(F32)[:, None] * inv_freq[None, :]
    cos = jnp.cos(ang)
    sin = jnp.sin(ang)
    return jnp.concatenate([cos, cos], axis=1), jnp.concatenate([-sin, sin], axis=1)


def _decay_tables(chunk):
    log_g = jnp.log1p(-jnp.exp2(-5.0 - jnp.arange(RET_HEADS, dtype=F32)))
    i = jnp.arange(chunk, dtype=F32)
    diff = i[:, None] - i[None, :]
    dmask = jnp.where(diff[None] >= 0, jnp.exp(jnp.maximum(diff, 0.0)[None] * log_g[:, None, None]), 0.0)
    xi = jnp.exp((i[None, :] + 1.0) * log_g[:, None])
    zeta = jnp.exp((chunk - 1.0 - i)[None, :] * log_g[:, None])
    gch = jnp.exp(chunk * log_g)
    hd = RET_HEAD_DIM
    bc = lambda t: jnp.broadcast_to(t[:, :, None], (RET_HEADS, chunk, hd))
    return dmask, bc(xi), bc(zeta), jnp.broadcast_to(gch[:, None, None], (RET_HEADS, 1, hd))


def _ssm_tables(prep, t_len):
    assert t_len in (SSM_CHUNK, SSM_CHUNK // 2)
    toep, bpre, bpim, cpre, cpim, lam = prep
    tn = t_len * SSM_GROUP
    off = (SSM_CHUNK - t_len) * SSM_GROUP
    bp = jnp.concatenate([bpre, bpim], axis=-1)[:, off:, :]
    bps = jnp.concatenate([bpim, bpre], axis=-1)[:, off:, :]
    cp = jnp.concatenate([cpre, cpim], axis=-1)[:, :tn, :]
    r = 0 if t_len == SSM_CHUNK else 2
    lr, li = lam[:, r:r + 1, :], lam[:, r + 1:r + 2, :]
    rows = jnp.concatenate([jnp.concatenate([lr, lr], -1), jnp.concatenate([-li, li], -1)], axis=1)
    lam8 = jnp.concatenate([rows, jnp.zeros((rows.shape[0], 6, rows.shape[2]), F32)], axis=1)
    return toep[:, :tn, :tn], bp, bps, cp, lam8


def _layer(x3d, past, x0_re, x0_im, s0, lw, prep, consts, layer):
    bsz, seq, _ = x3d.shape
    m = bsz * seq
    x2d = x3d.reshape(m, D_MODEL)
    tm_big = min(1024, m)

    u4, q2d, k2d, v2d, ret_in = _in_proj(x2d, lw["gmix"], lw["w_in"], lw["qg"], lw["kg"], consts["bd"], tm_big)

    t_len = min(SSM_CHUNK, seq)
    nchunk = seq // t_len
    n_seq = bsz if nchunk == 1 else 1
    nblk = bsz // n_seq
    g = SSM_GROUPS
    nslab = BLK // LANES
    x0 = jnp.stack([jnp.concatenate([x0_re, x0_im], axis=-1), jnp.concatenate([x0_im, x0_re], axis=-1)], axis=2)
    x0 = x0.reshape(nblk, n_seq, 2 * g, 2 * SSM_STATE).transpose(0, 2, 1, 3)
    toep, bp, bps, cp, lam8 = _ssm_tables(prep, t_len)
    dfull = jnp.tile(lw["ssm_d"], (1, t_len))[:, None, :]
    y4, xf = _ssm(u4.reshape(nslab, nblk, m // nblk, LANES), x0, toep, bp, bps, cp, lam8, dfull, n_seq, nchunk, t_len)
    y4 = y4.reshape(nslab, m, LANES)
    xf = xf.transpose(0, 2, 1, 3).reshape(bsz, g, 2 * SSM_STATE)
    ssm_re, ssm_im = xf[..., :SSM_STATE], xf[..., SSM_STATE:]

    if past is None:
        ob = _sb_prompt(lw["sb_bias"], q2d, k2d, v2d, _tri(min(SB_TILE, seq)), bsz, seq)
        start = 0
    else:
        cache_kt, cache_vt, page_table = past
        ob = _sb_paged(page_table, lw["sb_bias"], q2d, k2d, v2d, consts["tri_page"], cache_kt, cache_vt, layer, seq)
        start = page_table.shape[1] * PAGE_SIZE

    cosf, sinf = _rotary_tables(start, seq)
    dmask, xi, zeta, gch = _decay_tables(min(RET_CHUNK, seq))
    oc, ret_state = _retention(ret_in, cosf, sinf, dmask, xi, zeta, gch, s0, lw["ret_gain"], bsz, seq)

    x1 = _merge(x2d, lw["gmix"], y4, ob, oc, lw["wgate"], lw["bgate"], lw["wglu"], lw["wbs"], lw["wbb"], lw["wbr"],
                lw["wo"], min(256, m))
    x2 = _ffn(x1, lw["gffn"], lw["wup"], lw["wdn"], tm_big, 1024)
    shape_kv = (bsz, seq, SB_HEADS, SB_HEAD_DIM)
    return x2.reshape(bsz, seq, D_MODEL), k2d.reshape(shape_kv), v2d.reshape(shape_kv), ssm_re, ssm_im, ret_state


def kernel(x_prompt, x_sample, cache_k, cache_v, state_ssm_re, state_ssm_im, state_ret, page_table, norm_mix, w_in, sb_q_norm, sb_k_norm, sb_logit_bias, ssm_lambda_re, ssm_lambda_im, ssm_log_dt, ssm_b_re, ssm_b_im, ssm_c_re, ssm_c_im, ssm_d, ssm_w_glu, ret_norm, w_branch_ssm, w_branch_sb, w_branch_ret, w_gate, b_gate, w_o, norm_ffn, w_ff_up, w_ff_down):
    depth = w_in.shape[0]
    n_prompt = x_prompt.shape[0]
    n_phys = cache_k.shape[1]
    cache_kt = cache_k.transpose(0, 1, 3, 4, 2).reshape(depth, n_phys, BLK, PAGE_SIZE)
    cache_vt = cache_v.transpose(0, 1, 3, 4, 2).reshape(depth, n_phys, BLK, PAGE_SIZE)
    lane_head = jnp.arange(BLK) // SB_HEAD_DIM
    consts = {
        "bd": jnp.where(lane_head[:, None] == lane_head[None, :], 1.0 / SB_HEAD_DIM, 0.0).astype(BF16),
        "tri_page": _tri(PAGE_SIZE),
    }
    zeros_ssm = jnp.zeros((n_prompt, SSM_GROUPS, SSM_STATE), F32)
    zeros_ret = jnp.zeros((n_prompt, RET_HEADS, RET_HEAD_DIM, RET_HEAD_DIM), F32)

    yp, ys = x_prompt, x_sample
    outs_p, outs_s = [], []
    for l in range(depth):
        lw = {
            "gmix": norm_mix[l][None, :],
            "w_in": w_in[l].astype(BF16),
            "qg": jnp.tile(sb_q_norm[l], SB_HEADS)[None, :],
            "kg": jnp.tile(sb_k_norm[l], SB_HEADS)[None, :],
            "sb_bias": sb_logit_bias[l],
            "ssm_d": ssm_d[l],
            "ret_gain": ret_norm[l][None, :],
            "wgate": w_gate[l].astype(BF16),
            "bgate": b_gate[l][None, :],
            "wglu": ssm_w_glu[l].astype(BF16),
            "wbs": w_branch_ssm[l].astype(BF16),
            "wbb": w_branch_sb[l].astype(BF16),
            "wbr": w_branch_ret[l].astype(BF16),
            "wo": w_o[l].astype(BF16),
            "gffn": norm_ffn[l][None, :],
            "wup": w_ff_up[l].astype(BF16),
            "wdn": w_ff_down[l].astype(BF16),
        }
        prep = _ssm_prep(ssm_lambda_re[l][:, None, :], ssm_lambda_im[l][:, None, :], ssm_log_dt[l][:, None, None],
                         ssm_b_re[l].transpose(0, 2, 1), ssm_b_im[l].transpose(0, 2, 1), ssm_c_re[l], ssm_c_im[l])
        rp = _layer(yp, None, zeros_ssm, zeros_ssm, zeros_ret, lw, prep, consts, l)
        rs = _layer(ys, (cache_kt, cache_vt, page_table), state_ssm_re[l], state_ssm_im[l], state_ret[l],
                    lw, prep, consts, l)
        yp, ys = rp[0], rs[0]
        outs_p.append(rp[1:])
        outs_s.append(rs[1:])
    stack = lambda outs, i: jnp.stack([o[i] for o in outs])
    return (yp, ys) + tuple(stack(outs_p, i) for i in range(5)) + tuple(stack(outs_s, i) for i in range(5))
```

```python
import functools

import jax
import jax.numpy as jnp
from jax import lax
from jax.experimental import pallas as pl
from jax.experimental.pallas import tpu as pltpu

F32 = jnp.float32
BF16 = jnp.bfloat16

EPS = 1e-6
D_MODEL = 1024
SSM_GROUP = 16
SSM_GROUPS = 32
SSM_STATE = 64
SSM_CHUNK = 16
SB_HEAD_DIM = 64
SB_HEADS = 8
RET_HEAD_DIM = 128
RET_HEADS = 4
RET_CHUNK = 128
ROPE_BASE = 10000.0
D_FF = 4 * D_MODEL
PAGE_SIZE = 128
BLK = 512
IN_WIDTH = 8 * BLK
GATE_WIDTH = 3 * D_MODEL
LANES = 128
SB_TILE = 256
PAGES_PER_STEP = 8
PAGE_SLOTS = 3
VMEM_LIMIT = 56 * 1024 * 1024

_NT = (((1,), (1,)), ((), ()))
_TN = (((0,), (0,)), ((), ()))


def _dot(a, b):
    return jnp.dot(a, b, preferred_element_type=F32)


def _dot_nt(a, b):
    return lax.dot_general(a, b, _NT, preferred_element_type=F32)


def _split2(x):
    hi = x.astype(BF16)
    lo = (x - hi.astype(F32)).astype(BF16)
    return hi, lo


def _dot_x3(a, b, nt=False):
    d = _dot_nt if nt else _dot
    ah, al = _split2(a)
    bh, bl = _split2(b)
    return d(ah, bh) + (d(ah, bl) + d(al, bh))


def _div_pow2(x, d):
    assert d & (d - 1) == 0
    return lax.shift_right_logical(x, jnp.int32(d.bit_length() - 1))


def _softplus(z):
    return jnp.maximum(z, 0.0) + jnp.log(1.0 + jnp.exp(-jnp.abs(z)))


def _rms_norm(x, g):
    ms = jnp.mean(x * x, axis=-1, keepdims=True)
    return x * lax.rsqrt(ms + EPS) * g


def _params(sem):
    return pltpu.CompilerParams(dimension_semantics=sem, vmem_limit_bytes=VMEM_LIMIT)


def _in_proj_kernel(x_ref, gmix_ref, w_ref, qg_ref, kg_ref, bd_ref, k_all_ref, v_all_ref,
                    u4_ref, q_ref, k_ref, v_ref, ret_ref, h_scr):
    del k_all_ref, v_all_ref
    j = pl.program_id(1)

    @pl.when(j == 0)
    def _():
        h_scr[...] = _rms_norm(x_ref[...], gmix_ref[...]).astype(BF16)

    acc = _dot(h_scr[...], w_ref[...])

    def head_norm(a, g_ref):
        hi, lo = _split2(a * a)
        msq = _dot(hi, bd_ref[...]) + _dot(lo, bd_ref[...])
        return a * lax.rsqrt(msq + EPS) * g_ref[...]

    @pl.when(j == 0)
    def _():
        for s in range(BLK // LANES):
            u4_ref[s] = acc[:, s * LANES:(s + 1) * LANES]
        q_ref[...] = head_norm(acc[:, BLK:], qg_ref)

    @pl.when(j == 1)
    def _():
        k_ref[...] = head_norm(acc[:, :BLK], kg_ref)
        v_ref[...] = acc[:, BLK:]

    @pl.when(j >= 2)
    def _():
        ret_ref[...] = acc


def _in_proj(x2d, gmix, w_in, qg, kg, bd, k_all, v_all, layer, tm):
    m = x2d.shape[0]
    sds = jax.ShapeDtypeStruct
    nslab = BLK // LANES
    wide = 2 * BLK

    def resident(width):
        return pl.BlockSpec((tm, width), lambda i, j: (i, 0))

    layer_rows = pl.BlockSpec((None, tm, BLK), lambda i, j: (layer, i, 0))
    return pl.pallas_call(
        _in_proj_kernel,
        grid=(m // tm, IN_WIDTH // wide),
        in_specs=[
            pl.BlockSpec((tm, D_MODEL), lambda i, j: (i, 0)),
            pl.BlockSpec((1, D_MODEL), lambda i, j: (0, 0)),
            pl.BlockSpec((D_MODEL, wide), lambda i, j: (0, j)),
            pl.BlockSpec((1, BLK), lambda i, j: (0, 0)),
            pl.BlockSpec((1, BLK), lambda i, j: (0, 0)),
            pl.BlockSpec((BLK, BLK), lambda i, j: (0, 0)),
            pl.BlockSpec(memory_space=pl.ANY),
            pl.BlockSpec(memory_space=pl.ANY),
        ],
        out_specs=[
            pl.BlockSpec((nslab, tm, LANES), lambda i, j: (0, i, 0)),
            resident(BLK), layer_rows, layer_rows,
            pl.BlockSpec((tm, wide), lambda i, j: (i, jnp.maximum(j - 2, 0))),
        ],
        out_shape=[sds((nslab, m, LANES), F32), sds((m, BLK), F32), sds(k_all.shape, F32), sds(v_all.shape, F32),
                   sds((m, IN_WIDTH // 2), F32)],
        input_output_aliases={6: 2, 7: 3},
        scratch_shapes=[pltpu.VMEM((tm, D_MODEL), BF16)],
        compiler_params=_params(("parallel", "arbitrary")),
        name="in_proj",
    )(x2d, gmix, w_in, qg, kg, bd, k_all, v_all)


def _ssm_prep_kernel(lre_ref, lim_ref, ldt_ref, btre_ref, btim_ref, cre_ref, cim_ref,
                     toep_ref, bpre_ref, bpim_ref, cpre_ref, cpim_ref, lam_ref):
    t_len = SSM_CHUNK
    lre = lre_ref[...]
    lim = lim_ref[...]
    dt = jnp.exp(ldt_ref[...])
    a = lre * dt
    b = lim * dt
    ea = jnp.exp(a)
    lbr = ea * jnp.cos(b)
    lbi = ea * jnp.sin(b)
    den = lre * lre + lim * lim
    xr = lbr - 1.0
    cr = (xr * lre + lbi * lim) / den
    ci = (lbi * lre - xr * lim) / den
    btre = btre_ref[...]
    btim = btim_ref[...]
    bbr = cr * btre - ci * btim
    bbi = cr * btim + ci * btre
    cre = cre_ref[...]
    cim = cim_ref[...]

    one = jnp.ones_like(lre)
    zero = jnp.zeros_like(lre)
    pows = [(one, zero)]
    for _ in range(t_len):
        pr, pi = pows[-1]
        pows.append((pr * lbr - pi * lbi, pr * lbi + pi * lbr))

    def cmul(xre, xim, p):
        return xre * p[0] - xim * p[1], xre * p[1] + xim * p[0]

    def stack(parts):
        return jnp.concatenate(parts, axis=0)

    a_parts = [cmul(cre, cim, pows[t]) for t in range(t_len)]
    a1_parts = [cmul(cre, cim, pows[t + 1]) for t in range(t_len)]
    h_parts = [cmul(bbr, bbi, pows[t_len - 1 - s]) for s in range(t_len)]
    a_re = stack([p[0] for p in a_parts])
    a_im = stack([p[1] for p in a_parts])

    tn = t_len * SSM_GROUP
    r0 = _dot_x3(bbr, a_re, nt=True) - _dot_x3(bbi, a_im, nt=True)
    lane = lax.broadcasted_iota(jnp.int32, (SSM_GROUP, tn), 1)
    blocks = [r0]
    for s in range(1, t_len):
        blocks.append(jnp.where(lane >= s * SSM_GROUP, pltpu.roll(r0, s * SSM_GROUP, 1), 0.0))
    toep_ref[...] = stack(blocks)
    bpre_ref[...] = stack([p[0] for p in h_parts])
    bpim_ref[...] = stack([p[1] for p in h_parts])
    cpre_ref[...] = stack([p[0] for p in a1_parts])
    cpim_ref[...] = -stack([p[1] for p in a1_parts])
    half = t_len // 2
    lam_ref[...] = stack([pows[t_len][0], pows[t_len][1], pows[half][0], pows[half][1],
                          zero, zero, zero, zero])


def _ssm_prep(lre, lim, ldt, btre, btim, cre, cim):
    g = lre.shape[0]
    n, p = SSM_GROUP, SSM_STATE
    tn = SSM_CHUNK * n
    sds = jax.ShapeDtypeStruct

    def spec(r, c):
        return pl.BlockSpec((None, r, c), lambda i: (i, 0, 0))

    return pl.pallas_call(
        _ssm_prep_kernel,
        grid=(g,),
        in_specs=[spec(1, p), spec(1, p), spec(1, 1), spec(n, p), spec(n, p), spec(n, p), spec(n, p)],
        out_specs=[spec(tn, tn), spec(tn, p), spec(tn, p), spec(tn, p), spec(tn, p), spec(8, p)],
        out_shape=[sds((g, tn, tn), F32), sds((g, tn, p), F32), sds((g, tn, p), F32),
                   sds((g, tn, p), F32), sds((g, tn, p), F32), sds((g, 8, p), F32)],
        compiler_params=_params(("parallel",)),
        name="ssm_prep",
    )(lre, lim, ldt, btre, btim, cre, cim)


def _ssm_kernel(u_ref, x0_ref, toep_ref, bp_ref, bps_ref, cp_ref, lam_ref, d_ref, y_ref, xf_ref,
                w_scr, ws_scr, xall_scr, *, n_seq, n_chunks, t_len):
    assert n_seq == 1 or n_chunks == 1
    n = SSM_GROUP
    per = LANES // n
    halves = t_len // per
    lane_run = _div_pow2(lax.broadcasted_iota(jnp.int32, (1, LANES), 1), n)
    rows = n_seq * n_chunks
    pieces = [u_ref[pl.ds(s, rows, stride=t_len), :] for s in range(t_len)]

    def transpose_runs(vs):
        vs = list(vs)
        d = per // 2
        while d:
            low = (lane_run & d) == 0
            for i in range(per):
                if i & d == 0:
                    a, b = vs[i], vs[i + d]
                    vs[i] = jnp.where(low, a, pltpu.roll(b, d * n, 1))
                    vs[i + d] = jnp.where(low, pltpu.roll(a, LANES - d * n, 1), b)
            d //= 2
        return vs

    u_halves = [transpose_runs(pieces[k * per:(k + 1) * per]) for k in range(halves)]
    ugs = [u_halves[0][gl] if halves == 1 else jnp.concatenate([h[gl] for h in u_halves], axis=1)
           for gl in range(per)]

    for gl in range(per):
        w_scr[gl] = _dot_x3(ugs[gl], bp_ref[gl])
    for gl in range(per):
        ws_scr[gl] = _dot_x3(ugs[gl], bps_ref[gl])

    l1 = [jnp.broadcast_to(lam_ref[gl, 0:1, :], (n_seq, 2 * SSM_STATE)) for gl in range(per)]
    l2 = [jnp.broadcast_to(lam_ref[gl, 1:2, :], (n_seq, 2 * SSM_STATE)) for gl in range(per)]

    def step(rows, carry):
        out = []
        for gl in range(per):
            x, xs = carry[2 * gl], carry[2 * gl + 1]
            xall_scr[gl, rows, :] = x
            out.append(x * l1[gl] + xs * l2[gl] + w_scr[gl, rows, :])
            out.append(xs * l1[gl] - x * l2[gl] + ws_scr[gl, rows, :])
        return tuple(out)

    init = tuple(x0_ref[i] for i in range(2 * per))
    if n_chunks == 1:
        fin = step(slice(None), init)
    else:
        fin = lax.fori_loop(0, n_chunks, lambda c, carry: step(pl.ds(c, 1), carry), init)

    ys = []
    for gl in range(per):
        xf_ref[gl] = fin[2 * gl]
        y = (_dot_x3(ugs[gl], toep_ref[gl]) + _dot_x3(xall_scr[gl], cp_ref[gl], nt=True)
             + ugs[gl] * d_ref[gl])
        y = jax.nn.gelu(y)
        ys.append([y[:, k * LANES:(k + 1) * LANES] for k in range(halves)])
    for k in range(halves):
        back = transpose_runs([ys[gl][k] for gl in range(per)])
        for j in range(per):
            y_ref[pl.ds(k * per + j, rows, stride=t_len), :] = back[j]


def _ssm(u4, x0, toep, bp, bps, cp, lam, dfull, n_seq, n_chunks, t_len):
    nslab, nblk, tokens, width = u4.shape
    rows = n_seq * n_chunks
    per = LANES // SSM_GROUP
    p2 = 2 * SSM_STATE
    tn = t_len * SSM_GROUP
    sds = jax.ShapeDtypeStruct

    def table(*shape):
        return pl.BlockSpec((per,) + shape, lambda s, b: (s,) + (0,) * len(shape))

    kern = functools.partial(_ssm_kernel, n_seq=n_seq, n_chunks=n_chunks, t_len=t_len)
    return pl.pallas_call(
        kern,
        grid=(nslab, nblk),
        in_specs=[pl.BlockSpec((None, None, tokens, width), lambda s, b: (s, b, 0, 0)),
                  pl.BlockSpec((None, 2 * per, n_seq, p2), lambda s, b: (b, s, 0, 0)),
                  table(tn, tn), table(tn, p2), table(tn, p2), table(tn, p2), table(8, p2), table(1, tn)],
        out_specs=[pl.BlockSpec((None, None, tokens, width), lambda s, b: (s, b, 0, 0)),
                   pl.BlockSpec((None, per, n_seq, p2), lambda s, b: (b, s, 0, 0))],
        out_shape=[sds(u4.shape, F32), sds((nblk, nslab * per, n_seq, p2), F32)],
        scratch_shapes=[pltpu.VMEM((per, rows, p2), F32)] * 3,
        compiler_params=_params(("parallel", "parallel")),
        name="ssm",
    )(u4, x0, toep, bp, bps, cp, lam, dfull)


def _sb_logs(z, mask):
    sp = _softplus(z)
    lsig = z - sp
    if mask is not None:
        sp = jnp.where(mask, sp, 0.0)
    return lsig, sp, jnp.sum(sp, axis=1, keepdims=True)


def _sb_suffix(sp, tri, terms=2):
    if terms == 1:
        return _dot(sp.astype(BF16), tri)
    hi, lo = _split2(sp)
    return _dot(hi, tri) + _dot(lo, tri)


def _sb_weights(lsig, cs, carry, mask):
    att = jnp.exp(lsig - (cs + carry))
    if mask is not None:
        att = jnp.where(mask, att, 0.0)
    return att.astype(BF16)


def _sb_prompt_kernel(bias_ref, q_ref, k_ref, v_ref, tri_ref, o_ref, carry_scr, acc_scr, *, tile):
    hp = pl.program_id(1)
    qi = pl.program_id(2)
    q = q_ref[...] * (SB_HEAD_DIM ** -0.5)
    lane = lax.broadcasted_iota(jnp.int32, (1, LANES), 1)
    head_lanes = (lane < SB_HEAD_DIM, lane >= SB_HEAD_DIM)
    qh = [jnp.where(hl, q, 0.0).astype(BF16) for hl in head_lanes]
    bias = [bias_ref[hp * 2], bias_ref[hp * 2 + 1]]
    row = lax.broadcasted_iota(jnp.int32, (tile, tile), 0)
    col = lax.broadcasted_iota(jnp.int32, (tile, tile), 1)
    causal = col < row
    carry_scr[...] = jnp.zeros_like(carry_scr)
    acc_scr[...] = jnp.zeros_like(acc_scr)

    def blocks(kjs, mask):
        tri = tri_ref[...]
        carry = [carry_scr[0], carry_scr[1]]
        chains = [(i, hh) for i in range(len(kjs)) for hh in range(2)]
        k0s = [pl.multiple_of(kj * tile, tile) for kj in kjs]
        kbs = [k_ref[pl.ds(k0, tile), :].astype(BF16) for k0 in k0s]
        logs = [_sb_logs(_dot_nt(qh[hh], kbs[i]) + bias[hh], mask) for i, hh in chains]
        css = [_sb_suffix(lg[1], tri, terms=1) for lg in logs]
        atts = []
        for (i, hh), lg, cs in zip(chains, logs, css):
            atts.append(_sb_weights(lg[0], cs, jnp.concatenate([carry[hh]] * (tile // LANES), axis=1), mask))
            carry[hh] = carry[hh] + lg[2]
        out = None
        for (i, hh), att in zip(chains, atts):
            v = v_ref[pl.ds(k0s[i], tile), :]
            o = _dot(att, jnp.where(head_lanes[hh], v, 0.0).astype(BF16))
            out = o if out is None else out + o
        carry_scr[0] = carry[0]
        carry_scr[1] = carry[1]
        acc_scr[...] += out

    blocks([qi], causal)

    def body(t, _):
        blocks([qi - 1 - 2 * t, qi - 2 - 2 * t], None)
        return 0

    lax.fori_loop(0, lax.shift_right_logical(qi, 1), body, 0)

    @pl.when((qi & 1) == 1)
    def _():
        blocks([0], None)

    o_ref[...] = acc_scr[...]


def _sb_prompt(bias, q2d, k_all, v_all, tri, bsz, seq, layer):
    tile = min(SB_TILE, seq)
    nq = seq // tile
    m = bsz * seq
    kern = functools.partial(_sb_prompt_kernel, tile=tile)
    return pl.pallas_call(
        kern,
        grid=(bsz, SB_HEADS // 2, nq),
        in_specs=[
            pl.BlockSpec(memory_space=pltpu.SMEM),
            pl.BlockSpec((tile, LANES), lambda b, h, i: (b * nq + i, h)),
            pl.BlockSpec((None, seq, LANES), lambda b, h, i: (layer, b, h)),
            pl.BlockSpec((None, seq, LANES), lambda b, h, i: (layer, b, h)),
            pl.BlockSpec((tile, tile), lambda b, h, i: (0, 0)),
        ],
        out_specs=pl.BlockSpec((tile, LANES), lambda b, h, i: (b * nq + i, h)),
        out_shape=jax.ShapeDtypeStruct((m, BLK), F32),
        scratch_shapes=[pltpu.VMEM((2, tile, LANES), F32), pltpu.VMEM((tile, LANES), F32)],
        compiler_params=_params(("parallel", "parallel", "arbitrary")),
        name="sb_prompt",
    )(bias, q2d, k_all, v_all, tri)


def _sb_paged_kernel(pt_ref, bias_ref, q_ref, kn_ref, vn_ref, tri_ref, *rest, n_q, layer, n_seq, steps, n_pages):
    npg = PAGES_PER_STEP
    ck_hbm, cv_hbm, o_ref, kbuf, vbuf, sems, qbd_scr, bias_scr, carry_scr, acc_scr = rest
    rows = SB_HEADS * n_q
    s = pl.program_id(1)
    t = pl.program_id(0) * steps + s
    total = n_seq * steps

    def page_copies(tt):
        slot = lax.rem(tt, PAGE_SLOTS)
        bb = lax.div(tt, steps)
        ss = tt - bb * steps
        copies = []
        for i in range(npg):
            page = pt_ref[bb, n_pages - 1 - (ss * npg + i)]
            copies.append(pltpu.make_async_copy(ck_hbm.at[layer, page], kbuf.at[slot, i], sems.at[slot, 0]))
            copies.append(pltpu.make_async_copy(cv_hbm.at[layer, page], vbuf.at[slot, i], sems.at[slot, 1]))
        return copies

    @pl.when(t == 0)
    def _():
        for d in range(min(PAGE_SLOTS - 1, total)):
            for c in page_copies(jnp.int32(d)):
                c.start()

    @pl.when(t + (PAGE_SLOTS - 1) < total)
    def _():
        for c in page_copies(t + (PAGE_SLOTS - 1)):
            c.start()

    @pl.when(s == 0)
    def _():
        row_head = _div_pow2(lax.broadcasted_iota(jnp.int32, (rows, BLK), 0), n_q)
        lane_head = _div_pow2(lax.broadcasted_iota(jnp.int32, (rows, BLK), 1), SB_HEAD_DIM)
        q_all = jnp.concatenate([q_ref[...]] * SB_HEADS, axis=0) * (SB_HEAD_DIM ** -0.5)
        qbd = jnp.where(row_head == lane_head, q_all, 0.0).astype(BF16)
        qbd_scr[...] = qbd
        rh = _div_pow2(lax.broadcasted_iota(jnp.int32, (rows, LANES), 0), n_q)
        bt = jnp.zeros((rows, LANES), F32)
        for h in range(SB_HEADS):
            bt = jnp.where(rh == h, bias_ref[h], bt)
        bias_scr[...] = bt
        qpos = lax.broadcasted_iota(jnp.int32, (rows, PAGE_SIZE), 0) & (n_q - 1)
        kpos = lax.broadcasted_iota(jnp.int32, (rows, PAGE_SIZE), 1)
        mask = kpos < qpos
        pad = jnp.zeros((PAGE_SIZE - n_q, BLK), F32)
        kn = jnp.concatenate([kn_ref[...], pad], axis=0).astype(BF16)
        vn = jnp.concatenate([vn_ref[...], pad], axis=0).astype(BF16)
        z = _dot_nt(qbd, kn) + bt
        lsig, sp, tot = _sb_logs(z, mask)
        att = _sb_weights(lsig, _sb_suffix(sp, tri_ref[...]), 0.0, mask)
        acc_scr[...] = _dot(att, vn)
        carry_scr[...] = jnp.broadcast_to(tot, (rows, LANES))

    for c in page_copies(t):
        c.wait()
    slot = lax.rem(t, PAGE_SLOTS)
    k_pages = [kbuf[slot, i] for i in range(npg)]
    v_pages = [vbuf[slot, i] for i in range(npg)]

    kcat = jnp.concatenate([kp.astype(BF16) for kp in k_pages], axis=1)
    z = _dot(qbd_scr[...], kcat) + jnp.concatenate([bias_scr[...]] * npg, axis=1)
    sp = _softplus(z)
    lsig = z - sp
    sp_rows = jnp.concatenate([sp[:, i * PAGE_SIZE:(i + 1) * PAGE_SIZE] for i in range(npg)], axis=0)
    cs_rows = _sb_suffix(sp_rows, tri_ref[...])
    tot_rows = jnp.sum(sp_rows, axis=1, keepdims=True)
    carry = carry_scr[...]
    after = []
    for i in range(npg):
        after.append(cs_rows[i * rows:(i + 1) * rows] + carry)
        carry = carry + tot_rows[i * rows:(i + 1) * rows]
    carry_scr[...] = carry
    att = jnp.exp(lsig - jnp.concatenate(after, axis=1)).astype(BF16)
    vcat = jnp.concatenate([vp.astype(BF16) for vp in v_pages], axis=1)
    acc_scr[...] += _dot_nt(att, vcat)

    @pl.when(s == pl.num_programs(1) - 1)
    def _():
        lane_head = _div_pow2(lax.broadcasted_iota(jnp.int32, (n_q, BLK), 1), SB_HEAD_DIM)
        out = acc_scr[0:n_q, :]
        for h in range(1, SB_HEADS):
            out = jnp.where(lane_head == h, acc_scr[h * n_q:(h + 1) * n_q, :], out)
        o_ref[...] = out


def _sb_paged(page_table, bias, big, k_new, v_new, tri, cache_kt, cache_vt, layer, n_q):
    n_seq, n_pages = page_table.shape
    npg = PAGES_PER_STEP
    steps = n_pages // npg
    rows = SB_HEADS * n_q
    q_col = 0
    ring = (PAGE_SLOTS, npg, BLK, PAGE_SIZE)

    grid_spec = pltpu.PrefetchScalarGridSpec(
        num_scalar_prefetch=1,
        grid=(n_seq, steps),
        in_specs=[
            pl.BlockSpec(memory_space=pltpu.SMEM),
            pl.BlockSpec((n_q, BLK), lambda b, s, pt: (b, q_col)),
            pl.BlockSpec((n_q, BLK), lambda b, s, pt: (b, 0)),
            pl.BlockSpec((n_q, BLK), lambda b, s, pt: (b, 0)),
            pl.BlockSpec((PAGE_SIZE, PAGE_SIZE), lambda b, s, pt: (0, 0)),
            pl.BlockSpec(memory_space=pl.ANY),
            pl.BlockSpec(memory_space=pl.ANY),
        ],
        out_specs=pl.BlockSpec((n_q, BLK), lambda b, s, pt: (b, 0)),
        scratch_shapes=[pltpu.VMEM(ring, F32), pltpu.VMEM(ring, F32), pltpu.SemaphoreType.DMA((PAGE_SLOTS, 2)),
                        pltpu.VMEM((rows, BLK), BF16), pltpu.VMEM((rows, LANES), F32),
                        pltpu.VMEM((rows, LANES), F32), pltpu.VMEM((rows, BLK), F32)],
    )
    kern = functools.partial(_sb_paged_kernel, n_q=n_q, layer=layer, n_seq=n_seq, steps=steps, n_pages=n_pages)
    return pl.pallas_call(
        kern,
        grid_spec=grid_spec,
        out_shape=jax.ShapeDtypeStruct((n_seq * n_q, BLK), F32),
        compiler_params=_params(("arbitrary", "arbitrary")),
        name="sb_paged",
    )(page_table, bias, big, k_new, v_new, tri, cache_kt, cache_vt)


def _ret_kernel(q_ref, k_ref, v_ref, g_ref, cos_ref, sin_ref, dmask_ref, xi_ref, zeta_ref, gch_ref,
                s0_ref, gain_ref, o_ref, sout_ref, s_scr):
    c = pl.program_id(1)

    @pl.when(c == 0)
    def _():
        s_scr[...] = s0_ref[...]

    cos = cos_ref[...]
    sin = sin_ref[...]
    half = RET_HEAD_DIM // 2
    heads = range(RET_HEADS)
    sls = [slice(h * RET_HEAD_DIM, (h + 1) * RET_HEAD_DIM) for h in heads]
    rq_b, rk, v_b = [], [], []
    for h in heads:
        qh = q_ref[:, sls[h]]
        kh = k_ref[:, sls[h]]
        rq_b.append((qh * cos + pltpu.roll(qh, half, 1) * sin).astype(BF16))
        rk.append((kh * cos + pltpu.roll(kh, half, 1) * sin) * (RET_HEAD_DIM ** -0.5))
        v_b.append(v_ref[:, sls[h]].astype(BF16))
    states = [s_scr[h] for h in heads]
    inner = [(_dot_nt(rq_b[h], rk[h].astype(BF16)) * dmask_ref[h]).astype(BF16) for h in heads]
    cross = [_dot(rq_b[h], states[h].astype(BF16)) * xi_ref[h] for h in heads]
    outs = [_dot(inner[h], v_b[h]) + cross[h] for h in heads]
    for h in heads:
        kz = (rk[h] * zeta_ref[h]).astype(BF16)
        s_scr[h] = states[h] * gch_ref[h] + lax.dot_general(kz, v_b[h], _TN, preferred_element_type=F32)
    for h in heads:
        o = outs[h]
        mu = jnp.mean(o, axis=-1, keepdims=True)
        d = o - mu
        var = jnp.mean(d * d, axis=-1, keepdims=True)
        y = d * lax.rsqrt(var + EPS) * gain_ref[:, sls[h]]
        gate = g_ref[:, sls[h]]
        o_ref[:, sls[h]] = gate * jax.nn.sigmoid(gate) * y

    @pl.when(c == pl.num_programs(1) - 1)
    def _():
        sout_ref[...] = s_scr[...]


def _retention(big, cosf, sinf, dmask, xi, zeta, gch, s0, gain, bsz, seq):
    chunk = min(RET_CHUNK, seq)
    nc = seq // chunk
    m = bsz * seq
    hd = RET_HEAD_DIM
    sds = jax.ShapeDtypeStruct

    def col(j):
        return pl.BlockSpec((chunk, BLK), lambda b, c: (b * nc + c, j))

    def const3(shape):
        return pl.BlockSpec(shape, lambda b, c: (0, 0, 0))

    state_spec = pl.BlockSpec((None, RET_HEADS, hd, hd), lambda b, c: (b, 0, 0, 0))
    return pl.pallas_call(
        _ret_kernel,
        grid=(bsz, nc),
        in_specs=[
            col(0), col(1), col(2), col(3),
            pl.BlockSpec((chunk, hd), lambda b, c: (c, 0)),
            pl.BlockSpec((chunk, hd), lambda b, c: (c, 0)),
            const3((RET_HEADS, chunk, chunk)),
            const3((RET_HEADS, chunk, hd)),
            const3((RET_HEADS, chunk, hd)),
            const3((RET_HEADS, 1, hd)),
            state_spec,
            pl.BlockSpec((1, BLK), lambda b, c: (0, 0)),
        ],
        out_specs=[pl.BlockSpec((chunk, BLK), lambda b, c: (b * nc + c, 0)), state_spec],
        out_shape=[sds((m, BLK), F32), sds((bsz, RET_HEADS, hd, hd), F32)],
        scratch_shapes=[pltpu.VMEM((RET_HEADS, hd, hd), F32)],
        compiler_params=_params(("parallel", "arbitrary")),
        name="retention",
    )(big, big, big, big, cosf, sinf, dmask, xi, zeta, gch, s0, gain)


def _merge_kernel(x_ref, gmix_ref, ys_ref, ob_ref, oc_ref, wgate_ref, bgate_ref,
                  wglu_ref, wbs_ref, wbb_ref, wbr_ref, wo_ref, o_ref):
    x = x_ref[...]
    h = _rms_norm(x, gmix_ref[...]).astype(BF16)
    gates = jax.nn.sigmoid(_dot(h, wgate_ref[...]) + bgate_ref[...])
    ys = jnp.concatenate([ys_ref[s] for s in range(BLK // LANES)], axis=1)
    glu = _dot(ys.astype(BF16), wglu_ref[...])
    out_a = glu[:, :BLK] * jax.nn.sigmoid(glu[:, BLK:])
    d = D_MODEL
    merged = (gates[:, :d] * _dot(out_a.astype(BF16), wbs_ref[...])
              + gates[:, d:2 * d] * _dot(ob_ref[...].astype(BF16), wbb_ref[...])
              + gates[:, 2 * d:] * _dot(oc_ref[...].astype(BF16), wbr_ref[...]))
    o_ref[...] = x + _dot(merged.astype(BF16), wo_ref[...])


def _merge(x2d, gmix, ys, ob, oc, wgate, bgate, wglu, wbs, wbb, wbr, wo, tm):
    m = x2d.shape[0]

    def rows(width):
        return pl.BlockSpec((tm, width), lambda i: (i, 0))

    def whole(shape):
        return pl.BlockSpec(shape, lambda i: (0, 0), pipeline_mode=pl.Buffered(1))

    return pl.pallas_call(
        _merge_kernel,
        grid=(m // tm,),
        in_specs=[rows(D_MODEL), whole((1, D_MODEL)),
                  pl.BlockSpec((BLK // LANES, tm, LANES), lambda i: (0, i, 0)), rows(BLK), rows(BLK),
                  whole((D_MODEL, GATE_WIDTH)), whole((1, GATE_WIDTH)),
                  whole((BLK, 2 * BLK)), whole((BLK, D_MODEL)), whole((BLK, D_MODEL)), whole((BLK, D_MODEL)),
                  whole((D_MODEL, D_MODEL))],
        out_specs=rows(D_MODEL),
        out_shape=jax.ShapeDtypeStruct((m, D_MODEL), F32),
        compiler_params=_params(("parallel",)),
        name="merge",
    )(x2d, gmix, ys, ob, oc, wgate, bgate, wglu, wbs, wbb, wbr, wo)


def _ffn_kernel(x_ref, g_ref, wup_ref, wdn_ref, o_ref, h_scr, acc_scr):
    f = pl.program_id(1)

    @pl.when(f == 0)
    def _():
        x = x_ref[...]
        h_scr[...] = _rms_norm(x, g_ref[...]).astype(BF16)
        acc_scr[...] = x

    up = jnp.maximum(_dot(h_scr[...], wup_ref[...]), 0.0)
    acc_scr[...] += _dot((up * up).astype(BF16), wdn_ref[...])

    @pl.when(f == pl.num_programs(1) - 1)
    def _():
        o_ref[...] = acc_scr[...]


def _ffn(x2d, g, wup, wdn, tm, tf):
    m = x2d.shape[0]
    return pl.pallas_call(
        _ffn_kernel,
        grid=(m // tm, D_FF // tf),
        in_specs=[
            pl.BlockSpec((tm, D_MODEL), lambda i, f: (i, 0)),
            pl.BlockSpec((1, D_MODEL), lambda i, f: (0, 0)),
            pl.BlockSpec((D_MODEL, tf), lambda i, f: (0, f)),
            pl.BlockSpec((tf, D_MODEL), lambda i, f: (f, 0)),
        ],
        out_specs=pl.BlockSpec((tm, D_MODEL), lambda i, f: (i, 0)),
        out_shape=jax.ShapeDtypeStruct((m, D_MODEL), F32),
        scratch_shapes=[pltpu.VMEM((tm, D_MODEL), BF16), pltpu.VMEM((tm, D_MODEL), F32)],
        compiler_params=_params(("parallel", "arbitrary")),
        name="ffn",
    )(x2d, g, wup, wdn)


def _tri(keys):
    return (jnp.arange(keys)[:, None] > jnp.arange(keys)[None, :]).astype(BF16)


def _rotary_tables(start, seq):
    half = RET_HEAD_DIM // 2
    pos = start + jnp.arange(seq, dtype=jnp.int32)
    inv_freq = ROPE_BASE ** (-jnp.arange(half, dtype=F32) / half)
    ang = pos.astype(F32)[:, None] * inv_freq[None, :]
    cos = jnp.cos(ang)
    sin = jnp.sin(ang)
    return jnp.concatenate([cos, cos], axis=1), jnp.concatenate([-sin, sin], axis=1)


def _decay_tables(chunk):
    log_g = jnp.log1p(-jnp.exp2(-5.0 - jnp.arange(RET_HEADS, dtype=F32)))
    i = jnp.arange(chunk, dtype=F32)
    diff = i[:, None] - i[None, :]
    dmask = jnp.where(diff[None] >= 0, jnp.exp(jnp.maximum(diff, 0.0)[None] * log_g[:, None, None]), 0.0)
    xi = jnp.exp((i[None, :] + 1.0) * log_g[:, None])
    zeta = jnp.exp((chunk - 1.0 - i)[None, :] * log_g[:, None])
    gch = jnp.exp(chunk * log_g)
    hd = RET_HEAD_DIM
    bc = lambda t: jnp.broadcast_to(t[:, :, None], (RET_HEADS, chunk, hd))
    return dmask, bc(xi), bc(zeta), jnp.broadcast_to(gch[:, None, None], (RET_HEADS, 1, hd))


def _ssm_tables(prep, t_len):
    assert t_len in (SSM_CHUNK, SSM_CHUNK // 2)
    toep, bpre, bpim, cpre, cpim, lam = prep
    tn = t_len * SSM_GROUP
    off = (SSM_CHUNK - t_len) * SSM_GROUP
    bp = jnp.concatenate([bpre, bpim], axis=-1)[:, off:, :]
    bps = jnp.concatenate([bpim, bpre], axis=-1)[:, off:, :]
    cp = jnp.concatenate([cpre, cpim], axis=-1)[:, :tn, :]
    r = 0 if t_len == SSM_CHUNK else 2
    lr, li = lam[:, r:r + 1, :], lam[:, r + 1:r + 2, :]
    rows = jnp.concatenate([jnp.concatenate([lr, lr], -1), jnp.concatenate([-li, li], -1)], axis=1)
    lam8 = jnp.concatenate([rows, jnp.zeros((rows.shape[0], 6, rows.shape[2]), F32)], axis=1)
    return toep[:, :tn, :tn], bp, bps, cp, lam8


def _layer(x3d, past, x0_re, x0_im, s0, kv_all, lw, prep, consts, layer):
    bsz, seq, _ = x3d.shape
    m = bsz * seq
    x2d = x3d.reshape(m, D_MODEL)
    tm_big = min(1024, m)

    u4, q2d, k_all, v_all, ret_in = _in_proj(x2d, lw["gmix"], lw["w_in"], lw["qg"], lw["kg"], consts["bd"],
                                             kv_all[0], kv_all[1], layer, tm_big)

    t_len = min(SSM_CHUNK, seq)
    nchunk = seq // t_len
    n_seq = bsz if nchunk == 1 else 1
    nblk = bsz // n_seq
    g = SSM_GROUPS
    nslab = BLK // LANES
    x0 = jnp.stack([jnp.concatenate([x0_re, x0_im], axis=-1), jnp.concatenate([x0_im, x0_re], axis=-1)], axis=2)
    x0 = x0.reshape(nblk, n_seq, 2 * g, 2 * SSM_STATE).transpose(0, 2, 1, 3)
    toep, bp, bps, cp, lam8 = _ssm_tables(prep, t_len)
    dfull = jnp.tile(lw["ssm_d"], (1, t_len))[:, None, :]
    y4, xf = _ssm(u4.reshape(nslab, nblk, m // nblk, LANES), x0, toep, bp, bps, cp, lam8, dfull, n_seq, nchunk, t_len)
    y4 = y4.reshape(nslab, m, LANES)
    xf = xf.transpose(0, 2, 1, 3).reshape(bsz, g, 2 * SSM_STATE)
    ssm_re, ssm_im = xf[..., :SSM_STATE], xf[..., SSM_STATE:]

    if past is None:
        ob = _sb_prompt(lw["sb_bias"], q2d, k_all, v_all, _tri(min(SB_TILE, seq)), bsz, seq, layer)
        start = 0
    else:
        cache_kt, cache_vt, page_table = past
        ob = _sb_paged(page_table, lw["sb_bias"], q2d, k_all[layer], v_all[layer], consts["tri_page"],
                       cache_kt, cache_vt, layer, seq)
        start = page_table.shape[1] * PAGE_SIZE

    cosf, sinf = _rotary_tables(start, seq)
    dmask, xi, zeta, gch = _decay_tables(min(RET_CHUNK, seq))
    oc, ret_state = _retention(ret_in, cosf, sinf, dmask, xi, zeta, gch, s0, lw["ret_gain"], bsz, seq)

    x1 = _merge(x2d, lw["gmix"], y4, ob, oc, lw["wgate"], lw["bgate"], lw["wglu"], lw["wbs"], lw["wbb"], lw["wbr"],
                lw["wo"], min(256, m))
    x2 = _ffn(x1, lw["gffn"], lw["wup"], lw["wdn"], tm_big, 1024)
    return x2.reshape(bsz, seq, D_MODEL), (k_all, v_all), ssm_re, ssm_im, ret_state


def kernel(x_prompt, x_sample, cache_k, cache_v, state_ssm_re, state_ssm_im, state_ret, page_table, norm_mix, w_in, sb_q_norm, sb_k_norm, sb_logit_bias, ssm_lambda_re, ssm_lambda_im, ssm_log_dt, ssm_b_re, ssm_b_im, ssm_c_re, ssm_c_im, ssm_d, ssm_w_glu, ret_norm, w_branch_ssm, w_branch_sb, w_branch_ret, w_gate, b_gate, w_o, norm_ffn, w_ff_up, w_ff_down):
    depth = w_in.shape[0]
    n_prompt = x_prompt.shape[0]
    n_phys = cache_k.shape[1]
    cache_kt = cache_k.transpose(0, 1, 3, 4, 2).reshape(depth, n_phys, BLK, PAGE_SIZE)
    cache_vt = cache_v.transpose(0, 1, 3, 4, 2).reshape(depth, n_phys, BLK, PAGE_SIZE)
    lane_head = jnp.arange(BLK) // SB_HEAD_DIM
    consts = {
        "bd": jnp.where(lane_head[:, None] == lane_head[None, :], 1.0 / SB_HEAD_DIM, 0.0).astype(BF16),
        "tri_page": _tri(PAGE_SIZE),
    }
    zeros_ssm = jnp.zeros((n_prompt, SSM_GROUPS, SSM_STATE), F32)
    zeros_ret = jnp.zeros((n_prompt, RET_HEADS, RET_HEAD_DIM, RET_HEAD_DIM), F32)

    def kv_leaves(x):
        shape = (depth, x.shape[0] * x.shape[1], BLK)
        return jnp.zeros(shape, F32), jnp.zeros(shape, F32)

    kv_p, kv_s = kv_leaves(x_prompt), kv_leaves(x_sample)
    yp, ys = x_prompt, x_sample
    outs_p, outs_s = [], []
    for l in range(depth):
        lw = {
            "gmix": norm_mix[l][None, :],
            "w_in": w_in[l].astype(BF16),
            "qg": jnp.tile(sb_q_norm[l], SB_HEADS)[None, :],
            "kg": jnp.tile(sb_k_norm[l], SB_HEADS)[None, :],
            "sb_bias": sb_logit_bias[l],
            "ssm_d": ssm_d[l],
            "ret_gain": ret_norm[l][None, :],
            "wgate": w_gate[l].astype(BF16),
            "bgate": b_gate[l][None, :],
            "wglu": ssm_w_glu[l].astype(BF16),
            "wbs": w_branch_ssm[l].astype(BF16),
            "wbb": w_branch_sb[l].astype(BF16),
            "wbr": w_branch_ret[l].astype(BF16),
            "wo": w_o[l].astype(BF16),
            "gffn": norm_ffn[l][None, :],
            "wup": w_ff_up[l].astype(BF16),
            "wdn": w_ff_down[l].astype(BF16),
        }
        prep = _ssm_prep(ssm_lambda_re[l][:, None, :], ssm_lambda_im[l][:, None, :], ssm_log_dt[l][:, None, None],
                         ssm_b_re[l].transpose(0, 2, 1), ssm_b_im[l].transpose(0, 2, 1), ssm_c_re[l], ssm_c_im[l])
        rp = _layer(yp, None, zeros_ssm, zeros_ssm, zeros_ret, kv_p, lw, prep, consts, l)
        rs = _layer(ys, (cache_kt, cache_vt, page_table), state_ssm_re[l], state_ssm_im[l], state_ret[l], kv_s,
                    lw, prep, consts, l)
        yp, kv_p, ys, kv_s = rp[0], rp[1], rs[0], rs[1]
        outs_p.append(rp[2:])
        outs_s.append(rs[2:])
    stack = lambda outs, i: jnp.stack([o[i] for o in outs])

    def kv_out(kv, x):
        shape = (depth, x.shape[0], x.shape[1], SB_HEADS, SB_HEAD_DIM)
        return kv[0].reshape(shape), kv[1].reshape(shape)

    return ((yp, ys) + kv_out(kv_p, x_prompt) + tuple(stack(outs_p, i) for i in range(3))
            + kv_out(kv_s, x_sample) + tuple(stack(outs_s, i) for i in range(3)))
```
